```python
import math
import jax
import jax.numpy as jnp
from jax import lax
import numpy as np

D_MODEL = 1024
BATCH = 4
SEQ = 4096
DEPTH = 4
DEC_BATCH = 16
DEC_SEQ = 32
PAST_LEN = 1024

CHUNK = 64
D_A = D_MODEL
H_A = 8
DK_A = D_A // H_A
DV_A = D_A // H_A
SHORT_CONV = 4
C_B = D_MODEL
CONV_B = 31
H_C = 8
DH_C = 128
H_KV = 2
D_C = H_C * DH_C
H_I = 8
D_I = 64
TOPK = 256
ROPE_THETA = 500000.0
ROPE_FRACTION = 4
EPS = 1e-6
IN_SIZES = (3 * D_A, H_A, H_A, D_A, 2 * C_B, C_B, D_C, H_KV * DH_C, H_KV * DH_C, H_I * D_I, D_I, H_I, D_C, 3 * D_MODEL)
N_IN = 3 * D_A + 2 * H_A + D_A + 3 * C_B + 2 * D_C + 2 * H_KV * DH_C + H_I * D_I + D_I + H_I + 3 * D_MODEL

kernel_name = 'hybrid_stream_deltanet_conformer_dsa_step'


def rms_norm(x, g):
    x32 = x.astype(jnp.float32)
    y = x32 * lax.rsqrt(jnp.mean(x32 * x32, axis=-1, keepdims=True) + EPS)
    return (y * g.astype(jnp.float32)).astype(x.dtype)


def l2_norm(x):
    x32 = x.astype(jnp.float32)
    return (x32 * lax.rsqrt(jnp.sum(x32 * x32, axis=-1, keepdims=True) + EPS)).astype(x.dtype)


def layer_norm(x, g, b):
    x32 = x.astype(jnp.float32)
    mu = jnp.mean(x32, axis=-1, keepdims=True)
    xc = x32 - mu
    var = jnp.mean(xc * xc, axis=-1, keepdims=True)
    return (xc * lax.rsqrt(var + EPS) * g.astype(jnp.float32) + b.astype(jnp.float32)).astype(x.dtype)


def split_cols(t, sizes):
    offs = []
    acc = 0
    for s in sizes[:-1]:
        acc += s
        offs.append(acc)
    return jnp.split(t, offs, axis=-1)


def partial_rope(x, pos):
    d = x.shape[-1]
    rot = d // ROPE_FRACTION
    half = rot // 2
    inv = ROPE_THETA ** (-jnp.arange(half, dtype=jnp.float32) * 2.0 / rot)
    ang = pos.astype(jnp.float32)[:, None] * inv[None, :]
    cos = jnp.cos(ang)[None, :, None, :].astype(x.dtype)
    sin = jnp.sin(ang)[None, :, None, :].astype(x.dtype)
    x1, x2, xr = x[..., :half], x[..., half:rot], x[..., rot:]
    return jnp.concatenate([x1 * cos - x2 * sin, x2 * cos + x1 * sin, xr], axis=-1)


def causal_dwconv(x, hist, w):
    xp = jnp.concatenate([hist.astype(x.dtype), x], axis=1)
    c = x.shape[-1]
    y = lax.conv_general_dilated(xp, w[:, None, :].astype(x.dtype), window_strides=(1,), padding='VALID',
                                 dimension_numbers=('NWC', 'WIO', 'NWC'), feature_group_count=c)
    return y, xp[:, xp.shape[1] - (w.shape[0] - 1):]


def gated_delta_rule(q, k, v, g, beta, s0):
    bsz, t_len, nh, _ = q.shape
    dv = v.shape[-1]
    out_dtype = v.dtype
    c = CHUNK if t_len % CHUNK == 0 else t_len
    n = t_len // c
    f32 = jnp.float32

    def to_chunks(t):
        t = t.astype(f32).reshape((bsz, n, c, nh) + t.shape[3:])
        return jnp.moveaxis(t, 3, 1)

    q, k, v, g, beta = to_chunks(q), to_chunks(k), to_chunks(v), to_chunks(g), to_chunks(beta)
    big_g = jnp.cumsum(g, axis=-1)
    ar = jnp.arange(c)
    incl = ar[:, None] >= ar[None, :]
    strict = ar[:, None] > ar[None, :]
    diff = big_g[..., :, None] - big_g[..., None, :]
    dec_incl = jnp.exp(jnp.where(incl, diff, -jnp.inf))
    dec_strict = jnp.exp(jnp.where(strict, diff, -jnp.inf))
    kb = k * beta[..., None]
    a_mat = jnp.eye(c, dtype=f32) + jnp.einsum('bhnid,bhnjd->bhnij', kb, k) * dec_strict
    u = lax.linalg.triangular_solve(a_mat, v * beta[..., None], left_side=True, lower=True, unit_diagonal=True)
    w = lax.linalg.triangular_solve(a_mat, kb * jnp.exp(big_g)[..., None], left_side=True, lower=True, unit_diagonal=True)
    qk = jnp.einsum('bhnid,bhnjd->bhnij', q, k) * dec_incl
    q_dec = q * jnp.exp(big_g)[..., None]
    k_dec = k * jnp.exp(big_g[..., -1:] - big_g)[..., None]
    g_end = jnp.exp(big_g[..., -1])

    def step(s, xs):
        u_c, w_c, qk_c, qd_c, kd_c, ge_c = xs
        v_new = u_c - jnp.einsum('bhcd,bhde->bhce', w_c, s)
        o_c = jnp.einsum('bhcd,bhde->bhce', qd_c, s) + jnp.einsum('bhij,bhje->bhie', qk_c, v_new)
        s = ge_c[..., None, None] * s + jnp.einsum('bhcd,bhce->bhde', kd_c, v_new)
        return s, o_c

    xs = tuple(jnp.moveaxis(t, 2, 0) for t in (u, w, qk, q_dec, k_dec, g_end))
    s_fin, o = lax.scan(step, s0.astype(f32), xs)
    o = jnp.transpose(o, (1, 0, 3, 2, 4)).reshape(bsz, t_len, nh, dv)
    return o.astype(out_dtype), s_fin.astype(s0.dtype)


def dsa_attention(q, qi, wi, k, v, ki, q_pos):
    bsz, t_len = q.shape[0], q.shape[1]
    l_keys = k.shape[1]
    n_sel = min(TOPK, l_keys // 4)
    qb = 128 if t_len % 128 == 0 else t_len
    nb = t_len // qb
    k_chunk = jnp.arange(l_keys) // CHUNK
    group = H_C // H_KV

    def blocks(t):
        return t.reshape((bsz, nb, qb) + t.shape[2:]).swapaxes(0, 1)

    def one_block(args):
        qq, qqi, ww, qp = args
        q_chunk = qp // CHUNK
        adm = k_chunk[None, :] <= q_chunk[:, None]
        logits = jnp.einsum('bqjd,bsd->bqjs', qqi, ki).astype(jnp.float32)
        score = jnp.einsum('bqj,bqjs->bqs', ww.astype(jnp.float32), jax.nn.relu(logits))
        score = jnp.where(adm[None], score, -jnp.inf)
        _, idx = lax.top_k(score, n_sel)
        valid = k_chunk[idx] <= q_chunk[None, :, None]
        k_sel = jax.vmap(lambda kb_, ib_: kb_[ib_])(k, idx)
        v_sel = jax.vmap(lambda vb_, ib_: vb_[ib_])(v, idx)
        qg = qq.reshape(bsz, qb, H_KV, group, DH_C)
        s = jnp.einsum('bqhgd,bqshd->bqhgs', qg, k_sel).astype(jnp.float32) * (DH_C ** -0.5)
        s = jnp.where(valid[:, :, None, None, :], s, -jnp.inf)
        prob = jax.nn.softmax(s, axis=-1).astype(v.dtype)
        o = jnp.einsum('bqhgs,bqshd->bqhgd', prob, v_sel)
        return o.reshape(bsz, qb, H_C, DH_C)

    out = lax.map(one_block, (blocks(q), blocks(qi), blocks(wi), q_pos.reshape(nb, qb)))
    return out.swapaxes(0, 1).reshape(bsz, t_len, H_C, DH_C)


def hybrid_layer(x, p, conv_a_hist, delta_s, conv_b_hist, past_k, past_v, past_ki):
    bsz, t_len, _ = x.shape
    past_len = past_k.shape[1]
    pos = past_len + jnp.arange(t_len)
    h = rms_norm(x, p['norm_g'])
    proj = h @ p['w_in']
    (qkv_a, a_in, b_in, z_a, glu_b, z_b, q_c, k_c, v_c, qi_c, ki_c, wi_c, z_c, gate_in) = split_cols(proj, IN_SIZES)

    qkv_conv, conv_a_new = causal_dwconv(qkv_a, conv_a_hist, p['conv_a_w'])
    qkv_conv = jax.nn.silu(qkv_conv)
    q_a, k_a, v_a = jnp.split(qkv_conv, 3, axis=-1)
    q_a = l2_norm(q_a.reshape(bsz, t_len, H_A, DK_A)) * (DK_A ** -0.5)
    k_a = l2_norm(k_a.reshape(bsz, t_len, H_A, DK_A))
    v_a = v_a.reshape(bsz, t_len, H_A, DV_A)
    decay = -jnp.exp(p['a_log']) * jax.nn.softplus(a_in + p['dt_bias'])
    beta_a = jax.nn.sigmoid(b_in)
    o_a, delta_new = gated_delta_rule(q_a, k_a, v_a, decay, beta_a, delta_s)
    o_a = rms_norm(o_a, p['onorm_a_g']).reshape(bsz, t_len, D_A) * jax.nn.silu(z_a)
    y_a = o_a @ p['w_o_a']

    u_b = glu_b[..., :C_B] * jax.nn.sigmoid(glu_b[..., C_B:])
    u_b, conv_b_new = causal_dwconv(u_b, conv_b_hist, p['conv_b_w'])
    u_b = jax.nn.silu(layer_norm(u_b + p['conv_b_bias'], p['ln_b_g'], p['ln_b_b']))
    y_b = (u_b * jax.nn.silu(z_b)) @ p['w_pw2_b']

    q_c = partial_rope(q_c.reshape(bsz, t_len, H_C, DH_C), pos)
    k_c = partial_rope(k_c.reshape(bsz, t_len, H_KV, DH_C), pos)
    v_c = v_c.reshape(bsz, t_len, H_KV, DH_C)
    qi_c = partial_rope(qi_c.reshape(bsz, t_len, H_I, D_I), pos) * (D_I ** -0.5)
    ki_c = partial_rope(ki_c[:, :, None, :], pos)[:, :, 0, :]
    wi_c = wi_c * (H_I ** -0.5)
    k_all = jnp.concatenate([past_k.astype(k_c.dtype), k_c], axis=1)
    v_all = jnp.concatenate([past_v.astype(v_c.dtype), v_c], axis=1)
    ki_all = jnp.concatenate([past_ki.astype(ki_c.dtype), ki_c], axis=1)
    o_c = dsa_attention(q_c, qi_c, wi_c, k_all, v_all, ki_all, pos)
    y_c = (o_c.reshape(bsz, t_len, D_C) * jax.nn.silu(z_c)) @ p['w_o_c']

    m_a, m_b, m_c = jnp.split(jax.nn.sigmoid(gate_in), 3, axis=-1)
    out = (m_a * y_a + m_b * y_b + m_c * y_c) @ p['w_out']
    return x + out, (k_c, v_c, ki_c, conv_a_new, delta_new, conv_b_new)


def setup_inputs(seed: int = 0) -> dict:
    key = jax.random.key(seed)
    ks = jax.random.split(key, 24)
    f32 = jnp.float32
    nrm = jax.random.normal
    x_prompt = nrm(ks[0], (BATCH, SEQ, D_MODEL), f32)
    x_sample = nrm(ks[1], (DEC_BATCH, DEC_SEQ, D_MODEL), f32)
    cache_k = nrm(ks[2], (DEPTH, DEC_BATCH, PAST_LEN, H_KV, DH_C), f32)
    cache_v = nrm(ks[3], (DEPTH, DEC_BATCH, PAST_LEN, H_KV, DH_C), f32)
    cache_idx_k = nrm(ks[4], (DEPTH, DEC_BATCH, PAST_LEN, D_I), f32)
    state_conv_a = nrm(ks[5], (DEPTH, DEC_BATCH, SHORT_CONV - 1, 3 * D_A), f32)
    state_delta = 0.1 * nrm(ks[6], (DEPTH, DEC_BATCH, H_A, DK_A, DV_A), f32)
    state_conv_b = 0.5 * nrm(ks[7], (DEPTH, DEC_BATCH, CONV_B - 1, C_B), f32)
    norm_g = 1.0 + 0.02 * nrm(ks[8], (DEPTH, D_MODEL), f32)
    w_in = nrm(ks[9], (DEPTH, D_MODEL, N_IN), f32) * (D_MODEL ** -0.5)
    conv_a_w = nrm(ks[10], (DEPTH, SHORT_CONV, 3 * D_A), f32) * (SHORT_CONV ** -0.5)
    a_log = jnp.log(jax.random.uniform(ks[11], (DEPTH, H_A), f32, 1.0, 16.0))
    dt = jnp.exp(jax.random.uniform(ks[12], (DEPTH, H_A), f32, math.log(1e-3), math.log(1e-1)))
    dt_bias = dt + jnp.log(-jnp.expm1(-dt))
    onorm_a_g = 1.0 + 0.02 * nrm(ks[13], (DEPTH, DV_A), f32)
    w_o_a = nrm(ks[14], (DEPTH, D_A, D_MODEL), f32) * (D_A ** -0.5)
    conv_b_w = nrm(ks[15], (DEPTH, CONV_B, C_B), f32) * (CONV_B ** -0.5)
    conv_b_bias = 0.02 * nrm(ks[16], (DEPTH, C_B), f32)
    ln_b_g = 1.0 + 0.02 * nrm(ks[17], (DEPTH, C_B), f32)
    ln_b_b = 0.02 * nrm(ks[18], (DEPTH, C_B), f32)
    w_pw2_b = nrm(ks[19], (DEPTH, C_B, D_MODEL), f32) * (C_B ** -0.5)
    w_o_c = nrm(ks[20], (DEPTH, D_C, D_MODEL), f32) * (D_C ** -0.5)
    w_out = nrm(ks[21], (DEPTH, D_MODEL, D_MODEL), f32) * (D_MODEL ** -0.5)
    final_norm_g = 1.0 + 0.02 * nrm(ks[22], (D_MODEL,), f32)
    return {'x_prompt': x_prompt, 'x_sample': x_sample, 'cache_k': cache_k, 'cache_v': cache_v,
            'cache_idx_k': cache_idx_k, 'state_conv_a': state_conv_a, 'state_delta': state_delta,
            'state_conv_b': state_conv_b, 'norm_g': norm_g, 'w_in': w_in, 'conv_a_w': conv_a_w,
            'a_log': a_log, 'dt_bias': dt_bias, 'onorm_a_g': onorm_a_g, 'w_o_a': w_o_a,
            'conv_b_w': conv_b_w, 'conv_b_bias': conv_b_bias, 'ln_b_g': ln_b_g, 'ln_b_b': ln_b_b,
            'w_pw2_b': w_pw2_b, 'w_o_c': w_o_c, 'w_out': w_out, 'final_norm_g': final_norm_g}


def _stack(states, i):
    return jnp.stack([s[i] for s in states], axis=0)


def reference(x_prompt, x_sample, cache_k, cache_v, cache_idx_k, state_conv_a, state_delta, state_conv_b,
              norm_g, w_in, conv_a_w, a_log, dt_bias, onorm_a_g, w_o_a, conv_b_w, conv_b_bias,
              ln_b_g, ln_b_b, w_pw2_b, w_o_c, w_out, final_norm_g):
    dt = x_prompt.dtype
    bp = x_prompt.shape[0]
    zero_conv_a = jnp.zeros((bp, SHORT_CONV - 1, 3 * D_A), dt)
    zero_delta = jnp.zeros((bp, H_A, DK_A, DV_A), state_delta.dtype)
    zero_conv_b = jnp.zeros((bp, CONV_B - 1, C_B), dt)
    empty_kv = jnp.zeros((bp, 0, H_KV, DH_C), dt)
    empty_ki = jnp.zeros((bp, 0, D_I), dt)
    xp, xs = x_prompt, x_sample
    st_p, st_s = [], []
    for l in range(DEPTH):
        p = {'norm_g': norm_g[l], 'w_in': w_in[l], 'conv_a_w': conv_a_w[l], 'a_log': a_log[l],
             'dt_bias': dt_bias[l], 'onorm_a_g': onorm_a_g[l], 'w_o_a': w_o_a[l], 'conv_b_w': conv_b_w[l],
             'conv_b_bias': conv_b_bias[l], 'ln_b_g': ln_b_g[l], 'ln_b_b': ln_b_b[l], 'w_pw2_b': w_pw2_b[l],
             'w_o_c': w_o_c[l], 'w_out': w_out[l]}
        xp, sp = hybrid_layer(xp, p, zero_conv_a, zero_delta, zero_conv_b, empty_kv, empty_kv, empty_ki)
        xs, ss = hybrid_layer(xs, p, state_conv_a[l], state_delta[l], state_conv_b[l],
                              cache_k[l], cache_v[l], cache_idx_k[l])
        st_p.append(sp)
        st_s.append(ss)
    y_prompt = rms_norm(xp, final_norm_g)
    y_sample = rms_norm(xs, final_norm_g)
    new_k_prompt = _stack(st_p, 0)
    new_v_prompt = _stack(st_p, 1)
    new_idx_k_prompt = _stack(st_p, 2)
    conv_a_prompt = _stack(st_p, 3)
    delta_prompt = _stack(st_p, 4)
    conv_b_prompt = _stack(st_p, 5)
    new_k_sample = _stack(st_s, 0)
    new_v_sample = _stack(st_s, 1)
    new_idx_k_sample = _stack(st_s, 2)
    conv_a_sample = _stack(st_s, 3)
    delta_sample = _stack(st_s, 4)
    conv_b_sample = _stack(st_s, 5)
    return (y_prompt, y_sample, new_k_prompt, new_v_prompt, new_idx_k_prompt, conv_a_prompt, delta_prompt,
            conv_b_prompt, new_k_sample, new_v_sample, new_idx_k_sample, conv_a_sample, delta_sample,
            conv_b_sample)
```

```python
import functools
import math

import jax
import jax.numpy as jnp
from jax import lax
from jax.experimental import pallas as pl
from jax.experimental.pallas import tpu as pltpu

F32 = jnp.float32
BF16 = jnp.bfloat16
I32 = jnp.int32

D_MODEL = 1024
N_LAYERS = 4
CHUNK = 64
H_A = 8
DK_A = 128
SHORT_CONV = 4
C_B = 1024
CONV_B = 31
H_C = 8
DH_C = 128
H_KV = 2
H_I = 8
D_I = 64
TOPK = 256
ROPE_THETA = 500000.0
ROPE_FRACTION = 4
EPS = 1e-6
IN_SIZES = (3 * D_MODEL, H_A, H_A, D_MODEL, 2 * C_B, C_B, H_C * DH_C, H_KV * DH_C, H_KV * DH_C, H_I * D_I, D_I, H_I,
            H_C * DH_C, 3 * D_MODEL)

LANE = 128
HALO = 32

COL_QKV = 0
COL_GATE = 3072
COL_GLU = 6144
COL_ZA = 8192
COL_ZB = 9216
COL_QC = 10240
COL_ZC = 11264
COL_QI = 12288
COL_KC = 12800
COL_VC = 13056
COL_SMALL = 13312
NP = 13440
SM_A, SM_B, SM_WI, SM_KI = 0, 8, 16, 64

NEG_INF = float("-inf")
INT_MIN = -(2 ** 31)
NEG_INF_KEY = -2139095041

VMEM_LIMIT = 56 * 1024 * 1024


def _cparams(sem):
    return pltpu.CompilerParams(dimension_semantics=sem, vmem_limit_bytes=VMEM_LIMIT)


def _pick(n, cands):
    for c in cands:
        if n % c == 0:
            return c
    raise ValueError(f"no block size in {cands} divides {n}")


def _sigmoid(x):
    return jax.nn.sigmoid(x)


def _silu(x):
    return x * jax.nn.sigmoid(x)


def _inproj_kernel(x_ref, g_ref, w_ref, o_ref, xn_ref):
    @pl.when(pl.program_id(1) == 0)
    def _():
        x = x_ref[...]
        ms = jnp.mean(x * x, axis=-1, keepdims=True)
        xn_ref[...] = (x * lax.rsqrt(ms + EPS) * g_ref[0]).astype(BF16)

    o_ref[...] = jnp.dot(xn_ref[...], w_ref[0], preferred_element_type=F32)


def _inproj(x2d, g_all, w_all, layer):
    rows = x2d.shape[0]
    tm = _pick(rows, (512, 256, 128, 64, 32, 16, 8))
    tn = 1920
    return pl.pallas_call(
        _inproj_kernel,
        grid=(rows // tm, NP // tn),
        in_specs=[
            pl.BlockSpec((tm, D_MODEL), lambda i, j: (i, 0)),
            pl.BlockSpec((1, 1, D_MODEL), lambda i, j: (layer, 0, 0)),
            pl.BlockSpec((1, D_MODEL, tn), lambda i, j: (layer, 0, j)),
        ],
        out_specs=pl.BlockSpec((tm, tn), lambda i, j: (i, j)),
        out_shape=jax.ShapeDtypeStruct((rows, NP), F32),
        scratch_shapes=[pltpu.VMEM((tm, D_MODEL), BF16)],
        compiler_params=_cparams(("parallel", "arbitrary")),
        name="inproj",
    )(x2d, g_all, w_all)


def _rope_kernel(q_ref, k_ref, qi_ref, sm_ref, cq_ref, sq_ref, ci_ref, si_ref, qo_ref, ko_ref, qio_ref, kio_ref):
    tb = q_ref.shape[0]
    lane = lax.broadcasted_iota(I32, (tb, LANE), 1)
    l64 = lane & 63
    cq, sq, ci, si = cq_ref[...], sq_ref[...], ci_ref[...], si_ref[...]

    def rot_qk(x):
        sw = jnp.where(lane < 16, pltpu.roll(x, LANE - 16, 1), pltpu.roll(x, 16, 1))
        return jnp.where(lane < 32, x * cq + sw * sq, x)

    def rot_idx(x):
        sw = jnp.where(l64 < 8, pltpu.roll(x, LANE - 8, 1), pltpu.roll(x, 8, 1))
        return jnp.where(l64 < 16, x * ci + sw * si, x)

    for h in range(H_C):
        sl = slice(h * LANE, (h + 1) * LANE)
        qo_ref[:, sl] = rot_qk(q_ref[:, sl]).astype(BF16)
    for h in range(H_KV):
        sl = slice(h * LANE, (h + 1) * LANE)
        ko_ref[0, :, sl] = rot_qk(k_ref[:, sl])
    for h in range(H_I * D_I // LANE):
        sl = slice(h * LANE, (h + 1) * LANE)
        qio_ref[:, sl] = (rot_idx(qi_ref[:, sl]) * (D_I ** -0.5)).astype(BF16)
    kio_ref[0] = rot_idx(sm_ref[...])[:, SM_KI:SM_KI + D_I]


def _rope_tables(t_len, pos0):
    pos = (pos0 + jnp.arange(t_len)).astype(F32)

    def table(d):
        rot = d // ROPE_FRACTION
        half = rot // 2
        inv = ROPE_THETA ** (-jnp.arange(half, dtype=F32) * 2.0 / rot)
        ang = pos[:, None] * inv[None, :]
        cos, sin = jnp.cos(ang), jnp.sin(ang)
        c = jnp.concatenate([cos, cos, jnp.ones((t_len, d - rot), F32)], axis=1)
        s = jnp.concatenate([-sin, sin, jnp.zeros((t_len, d - rot), F32)], axis=1)
        return jnp.tile(c, (1, LANE // d)), jnp.tile(s, (1, LANE // d))

    cq, sq = table(DH_C)
    ci, si = table(D_I)
    return cq, sq, ci, si


def _rope(proj, bsz, t_len, pos0):
    tb = _pick(t_len, (512, 256, 128, 64, 32))
    nb = t_len // tb
    cq, sq, ci, si = _rope_tables(t_len, pos0)
    rows = bsz * t_len
    tab = pl.BlockSpec((tb, LANE), lambda b, j: (j, 0))
    return pl.pallas_call(
        _rope_kernel,
        grid=(bsz, nb),
        in_specs=[
            pl.BlockSpec((tb, 1024), lambda b, j: (b * nb + j, COL_QC // 1024)),
            pl.BlockSpec((tb, 256), lambda b, j: (b * nb + j, COL_KC // 256)),
            pl.BlockSpec((tb, 512), lambda b, j: (b * nb + j, COL_QI // 512)),
            pl.BlockSpec((tb, LANE), lambda b, j: (b * nb + j, COL_SMALL // LANE)),
            tab, tab, tab, tab,
        ],
        out_specs=[
            pl.BlockSpec((tb, 1024), lambda b, j: (b * nb + j, 0)),
            pl.BlockSpec((1, tb, 256), lambda b, j: (b, j, 0)),
            pl.BlockSpec((tb, 512), lambda b, j: (b * nb + j, 0)),
            pl.BlockSpec((1, tb, D_I), lambda b, j: (b, j, 0)),
        ],
        out_shape=[
            jax.ShapeDtypeStruct((rows, 1024), BF16),
            jax.ShapeDtypeStruct((bsz, t_len, 256), F32),
            jax.ShapeDtypeStruct((rows, 512), BF16),
            jax.ShapeDtypeStruct((bsz, t_len, D_I), F32),
        ],
        compiler_params=_cparams(("parallel", "parallel")),
        name="rope",
    )(proj, proj, proj, proj, cq, sq, ci, si)


def _mm(a, b, exact):
    if exact:
        return jnp.dot(a, b, precision=lax.Precision.HIGHEST, preferred_element_type=F32)
    return jnp.dot(a.astype(BF16), b.astype(BF16), preferred_element_type=F32)


def _mm_nt(a, b):
    return lax.dot_general(a.astype(BF16), b.astype(BF16), (((1,), (1,)), ((), ())), preferred_element_type=F32)


def _mm_tn(a, b):
    return lax.dot_general(a.astype(BF16), b.astype(BF16), (((0,), (0,)), ((), ())), preferred_element_type=F32)


def _delta_kernel(alog_ref, dtb_ref, q_ref, k_ref, v_ref, z_ref, sm_ref, hq_ref, hk_ref, hv_ref, wq_ref, wk_ref,
                  wv_ref, og_ref, s0_ref, o_ref, sfin_ref, s_sc, xq_sc, xk_sc, xv_sc, u_sc, w_sc, qd_sc, kd_sc,
                  qk_sc, ge_sc, *, c, tb, hb):
    hg = pl.program_id(1)
    tblk = pl.program_id(2)
    nc = tb // c
    n_sq = int(math.log2(c)) - 1

    @pl.when(tblk == 0)
    def _():
        s_sc[...] = s0_ref[0]
        xq_sc[0:8, :] = hq_ref[0]
        xk_sc[0:8, :] = hk_ref[0]
        xv_sc[0:8, :] = hv_ref[0]

    @pl.when(tblk > 0)
    def _():
        xq_sc[0:8, :] = xq_sc[tb:tb + 8, :]
        xk_sc[0:8, :] = xk_sc[tb:tb + 8, :]
        xv_sc[0:8, :] = xv_sc[tb:tb + 8, :]

    xq_sc[8:8 + tb, :] = q_ref[...]
    xk_sc[8:8 + tb, :] = k_ref[...]
    xv_sc[8:8 + tb, :] = v_ref[...]

    lane = lax.broadcasted_iota(I32, (c, LANE), 1)
    ri = lax.broadcasted_iota(I32, (c, c), 0)
    cj = lax.broadcasted_iota(I32, (c, c), 1)
    tri_incl = ri >= cj
    tri_strict = ri > cj
    ltri = jnp.where(tri_incl, 1.0, 0.0).astype(F32)
    eye = jnp.where(ri == cj, 1.0, 0.0).astype(F32)

    def conv_silu(x_sc, w_ref, r0, sl):
        win = x_sc[pl.ds(r0, c + 8), sl]
        acc = win[8:, :] * w_ref[SHORT_CONV - 1:SHORT_CONV, sl]
        for jj in range(SHORT_CONV - 1):
            sh = SHORT_CONV - 1 - jj
            acc = acc + pltpu.roll(win, sh, 0)[8:, :] * w_ref[jj:jj + 1, sl]
        return _silu(acc)

    def l2n(x):
        return x * lax.rsqrt(jnp.sum(x * x, axis=-1, keepdims=True) + EPS)

    def phase1(ci, carry):
        r0 = pl.multiple_of(ci * c, c)
        sm = sm_ref[pl.ds(r0, c), :]
        for hh in range(hb):
            h = hg * hb + hh
            sl = slice(hh * LANE, (hh + 1) * LANE)
            q = l2n(conv_silu(xq_sc, wq_ref, r0, sl)) * (DK_A ** -0.5)
            k = l2n(conv_silu(xk_sc, wk_ref, r0, sl))
            v = conv_silu(xv_sc, wv_ref, r0, sl)
            a_col = jnp.sum(jnp.where(lane == SM_A + h, sm, 0.0), axis=1, keepdims=True)
            b_col = jnp.sum(jnp.where(lane == SM_B + h, sm, 0.0), axis=1, keepdims=True)
            a_b = jnp.broadcast_to(a_col, (c, LANE)) + dtb_ref[h]
            softplus = jnp.maximum(a_b, 0.0) + jnp.log1p(jnp.exp(-jnp.abs(a_b)))
            g_b = -jnp.exp(jnp.full((c, LANE), alog_ref[h], F32)) * softplus
            beta_b = jnp.broadcast_to(_sigmoid(b_col), (c, LANE))
            big_g = _mm(ltri, g_b, True)
            g_row = jnp.sum(eye * big_g[:, :c], axis=0, keepdims=True)
            diff = jnp.minimum(big_g[:, :c] - g_row, 0.0)
            edec = jnp.exp(diff)
            dec_incl = jnp.where(tri_incl, edec, 0.0)
            dec_strict = jnp.where(tri_strict, edec, 0.0)
            kb = k * beta_b
            m = -(_mm_nt(kb, k) * dec_strict)
            p = eye + m
            mp = m
            for _ in range(n_sq):
                mp = _mm(mp, mp, True)
                p = p + _mm(p, mp, True)
            eg = jnp.exp(big_g)
            u = _mm(p, v * beta_b, True)
            w = _mm(p, kb * eg, True)
            g_last = big_g[c - 1:c, :]
            u_sc[hh, pl.ds(r0, c), :] = u
            w_sc[hh, pl.ds(r0, c), :] = w
            qd_sc[hh, pl.ds(r0, c), :] = q * eg
            kd_sc[hh, pl.ds(r0, c), :] = k * jnp.exp(g_last - big_g)
            qk_sc[hh, pl.ds(r0, c), :] = _mm_nt(q, k) * dec_incl
            ge_sc[hh, pl.ds(pl.multiple_of(ci * 8, 8), 8), :] = jnp.broadcast_to(jnp.exp(g_last), (8, LANE))
        return carry

    lax.fori_loop(0, nc, phase1, 0)

    og = og_ref[...]

    def phase2(ci, carry):
        r0 = pl.multiple_of(ci * c, c)
        for hh in range(hb):
            sl = slice(hh * LANE, (hh + 1) * LANE)
            s = s_sc[hh]
            ge = ge_sc[hh, pl.ds(pl.multiple_of(ci * 8, 8), 8), :][0:1, :]
            v_new = u_sc[hh, pl.ds(r0, c), :] - _mm(w_sc[hh, pl.ds(r0, c), :], s, False)
            o = _mm(qd_sc[hh, pl.ds(r0, c), :], s, False) + _mm(qk_sc[hh, pl.ds(r0, c), :], v_new, False)
            s_sc[hh] = ge * s + _mm_tn(kd_sc[hh, pl.ds(r0, c), :], v_new)
            o = o * lax.rsqrt(jnp.mean(o * o, axis=-1, keepdims=True) + EPS) * og
            o_ref[pl.ds(r0, c), sl] = (o * _silu(z_ref[pl.ds(r0, c), sl])).astype(BF16)
        return carry

    lax.fori_loop(0, nc, phase2, 0)
    sfin_ref[0] = s_sc[...]


def _delta(proj, bsz, t_len, hist8, s0, conv_w, a_log, dt_bias, onorm_g, *, tb, hb):
    c = CHUNK if t_len % CHUNK == 0 else t_len
    nb = t_len // tb
    hw = hb * LANE
    rows = bsz * t_len

    def col(off):
        return lambda b, g, t: (b * nb + t, off // hw + g)

    def hcol(off):
        return lambda b, g, t: (b, 0, off // hw + g)

    def wcol(off):
        return lambda b, g, t: (0, off // hw + g)

    smem = pl.BlockSpec(memory_space=pltpu.SMEM)
    kern = functools.partial(_delta_kernel, c=c, tb=tb, hb=hb)
    return pl.pallas_call(
        kern,
        grid=(bsz, H_A // hb, nb),
        in_specs=[
            smem, smem,
            pl.BlockSpec((tb, hw), col(COL_QKV)),
            pl.BlockSpec((tb, hw), col(COL_QKV + 1024)),
            pl.BlockSpec((tb, hw), col(COL_QKV + 2048)),
            pl.BlockSpec((tb, hw), col(COL_ZA)),
            pl.BlockSpec((tb, LANE), lambda b, g, t: (b * nb + t, COL_SMALL // LANE)),
            pl.BlockSpec((1, 8, hw), hcol(0)),
            pl.BlockSpec((1, 8, hw), hcol(1024)),
            pl.BlockSpec((1, 8, hw), hcol(2048)),
            pl.BlockSpec((SHORT_CONV, hw), wcol(0)),
            pl.BlockSpec((SHORT_CONV, hw), wcol(1024)),
            pl.BlockSpec((SHORT_CONV, hw), wcol(2048)),
            pl.BlockSpec((1, LANE), lambda b, g, t: (0, 0)),
            pl.BlockSpec((1, hb, DK_A, DK_A), lambda b, g, t: (b, g, 0, 0)),
        ],
        out_specs=[
            pl.BlockSpec((tb, hw), lambda b, g, t: (b * nb + t, g)),
            pl.BlockSpec((1, hb, DK_A, DK_A), lambda b, g, t: (b, g, 0, 0)),
        ],
        out_shape=[
            jax.ShapeDtypeStruct((rows, 1024), BF16),
            jax.ShapeDtypeStruct((bsz, H_A, DK_A, DK_A), F32),
        ],
        scratch_shapes=[
            pltpu.VMEM((hb, DK_A, DK_A), F32),
            pltpu.VMEM((tb + 8, hw), F32),
            pltpu.VMEM((tb + 8, hw), F32),
            pltpu.VMEM((tb + 8, hw), F32),
            pltpu.VMEM((hb, tb, LANE), F32),
            pltpu.VMEM((hb, tb, LANE), F32),
            pltpu.VMEM((hb, tb, LANE), F32),
            pltpu.VMEM((hb, tb, LANE), F32),
            pltpu.VMEM((hb, tb, c), F32),
            pltpu.VMEM((hb, (tb // c) * 8, LANE), F32),
        ],
        compiler_params=_cparams(("parallel", "parallel", "arbitrary")),
        name="delta",
    )(a_log, dt_bias, proj, proj, proj, proj, proj, hist8, hist8, hist8, conv_w, conv_w, conv_w, onorm_g, s0)


def _convb_kernel(glu_ref, halo_ref, hist_ref, z_ref, w_ref, bias_ref, lng_ref, lnb_ref, o_ref, tail_ref, xp_ref,
                  y_ref, *, tb):
    j = pl.program_id(1)

    @pl.when(j == 0)
    def _():
        xp_ref[0:HALO, :] = hist_ref[0]

    @pl.when(j > 0)
    def _():
        hl = halo_ref[...]
        xp_ref[0:HALO, :] = hl[:, :C_B] * _sigmoid(hl[:, C_B:])

    gl = glu_ref[...]
    xp_ref[HALO:HALO + tb, :] = gl[:, :C_B] * _sigmoid(gl[:, C_B:])
    tail_ref[0] = xp_ref[tb:tb + HALO, :]

    rc_n = min(tb, 64)
    cc_n = 256
    base = HALO - (CONV_B - 1)
    for cc in range(C_B // cc_n):
        cs = slice(cc * cc_n, (cc + 1) * cc_n)
        for rc in range(tb // rc_n):
            acc = jnp.zeros((rc_n, cc_n), F32)
            for k in range(CONV_B):
                r = rc * rc_n + base + k
                acc = acc + w_ref[k:k + 1, cs] * xp_ref[r:r + rc_n, cs]
            y_ref[rc * rc_n:(rc + 1) * rc_n, cs] = acc

    y = y_ref[...] + bias_ref[...]
    mu = jnp.mean(y, axis=-1, keepdims=True)
    yc = y - mu
    var = jnp.mean(yc * yc, axis=-1, keepdims=True)
    yn = yc * lax.rsqrt(var + EPS) * lng_ref[...] + lnb_ref[...]
    o_ref[...] = (_silu(yn) * _silu(z_ref[...])).astype(BF16)


def _convb(proj, bsz, t_len, hist32, w32, bias, ln_g, ln_b, *, tb):
    nb = t_len // tb
    rows = bsz * t_len
    rb = tb // HALO
    vec = pl.BlockSpec((1, C_B), lambda b, j: (0, 0))
    kern = functools.partial(_convb_kernel, tb=tb)
    return pl.pallas_call(
        kern,
        grid=(bsz, nb),
        in_specs=[
            pl.BlockSpec((tb, 2 * C_B), lambda b, j: (b * nb + j, COL_GLU // (2 * C_B))),
            pl.BlockSpec((HALO, 2 * C_B), lambda b, j: (jnp.maximum((b * nb + j) * rb - 1, 0), COL_GLU // (2 * C_B))),
            pl.BlockSpec((1, HALO, C_B), lambda b, j: (b, 0, 0)),
            pl.BlockSpec((tb, C_B), lambda b, j: (b * nb + j, COL_ZB // C_B)),
            pl.BlockSpec((HALO, C_B), lambda b, j: (0, 0)),
            vec, vec, vec,
        ],
        out_specs=[
            pl.BlockSpec((tb, C_B), lambda b, j: (b * nb + j, 0)),
            pl.BlockSpec((1, HALO, C_B), lambda b, j: (b, 0, 0)),
        ],
        out_shape=[
            jax.ShapeDtypeStruct((rows, C_B), BF16),
            jax.ShapeDtypeStruct((bsz, HALO, C_B), F32),
        ],
        scratch_shapes=[pltpu.VMEM((HALO + tb, C_B), F32), pltpu.VMEM((tb, C_B), F32)],
        compiler_params=_cparams(("parallel", "arbitrary")),
        name="convb",
    )(proj, proj, hist32, proj, w32, bias, ln_g, ln_b)


def _tile_lanes(x, n):
    return x if n == 1 else jnp.concatenate([x] * n, axis=1)


def _dsa_kernel(qi_ref, sm_ref, kit_ref, q_ref, k_ref, v_ref, z_ref, o_ref, key_ref, wb_ref, m_ref, l_ref, acc_ref,
                *, tq, kb, n_keys, n_valid, pos0, topk):
    i = pl.program_id(1)
    nrep = kb // LANE
    last_pos = pos0 + (i + 1) * tq - 1
    lim_last = jnp.minimum((last_pos // CHUNK + 1) * CHUNK, n_valid)
    nkb = jnp.minimum((lim_last + kb - 1) // kb, n_keys // kb)
    qpos = pos0 + i * tq + lax.broadcasted_iota(I32, (tq, 1), 0)
    lim_row = jnp.minimum(((qpos >> 6) + 1) << 6, n_valid)
    lane_kb = lax.broadcasted_iota(I32, (1, kb), 1)
    lane_tq = lax.broadcasted_iota(I32, (tq, LANE), 1)

    sm = sm_ref[...]
    for j in range(H_I):
        wb_ref[j] = jnp.broadcast_to(sm[:, SM_WI + j:SM_WI + j + 1] * (H_I ** -0.5), (tq, LANE))

    def p1(kblk, carry):
        kt = kit_ref[0, kblk]
        acc = jnp.zeros((tq, kb), F32)
        for j in range(H_I):
            lg = jnp.dot(qi_ref[:, j * D_I:(j + 1) * D_I], kt, preferred_element_type=F32)
            acc = acc + _tile_lanes(wb_ref[j], nrep) * jnp.maximum(lg, 0.0)
        adm = (kblk * kb + lane_kb) < lim_row
        bits = lax.bitcast_convert_type(jnp.where(adm, acc, NEG_INF), I32)
        key_ref[kblk] = bits ^ ((bits >> 31) & 0x7FFFFFFF)
        return carry

    lax.fori_loop(0, nkb, p1, 0)

    def count_ge(cand):
        cb = jnp.broadcast_to(cand, (tq, LANE))

        def body(kblk, cnt):
            blk = key_ref[kblk]
            for t in range(nrep):
                cnt = cnt + jnp.where(blk[:, t * LANE:(t + 1) * LANE] >= cb, 1.0, 0.0)
            return cnt

        cnt = lax.fori_loop(0, nkb, body, jnp.zeros((tq, LANE), F32))
        return jnp.sum(cnt, axis=1, keepdims=True)

    def bisect(p, t_u):
        cand_u = t_u | jnp.left_shift(jnp.int32(1), 31 - p)
        cnt = count_ge(cand_u ^ INT_MIN)
        return jnp.where(cnt >= topk, cand_u, t_u)

    t_u = lax.fori_loop(0, 32, bisect, jnp.zeros((tq, 1), I32))
    thr = jnp.maximum(t_u ^ INT_MIN, NEG_INF_KEY + 1)
    cnt_ge = count_ge(thr)

    @pl.when(jnp.max(cnt_ge) > topk)
    def _():
        need = topk - count_ge(thr + 1)
        thr_b = jnp.broadcast_to(thr, (tq, LANE))

        def count_tie_lt(cidx):
            cb = jnp.broadcast_to(cidx, (tq, LANE))

            def body(kblk, cnt):
                blk = key_ref[kblk]
                for t in range(nrep):
                    idx = kblk * kb + t * LANE + lane_tq
                    hit = (blk[:, t * LANE:(t + 1) * LANE] == thr_b) & (idx < cb)
                    cnt = cnt + jnp.where(hit, 1.0, 0.0)
                return cnt

            cnt = lax.fori_loop(0, nkb, body, jnp.zeros((tq, LANE), F32))
            return jnp.sum(cnt, axis=1, keepdims=True)

        nbits = max(n_keys - 1, 1).bit_length()

        def bis_idx(p, cm):
            cand = cm | jnp.left_shift(jnp.int32(1), nbits - 1 - p)
            return jnp.where(count_tie_lt(cand) < need, cand, cm)

        cm = lax.fori_loop(0, nbits, bis_idx, jnp.zeros((tq, 1), I32))
        cm_b = jnp.broadcast_to(cm, (tq, LANE))

        def rewrite(kblk, carry):
            blk = key_ref[kblk]
            parts = []
            for t in range(nrep):
                idx = kblk * kb + t * LANE + lane_tq
                part = blk[:, t * LANE:(t + 1) * LANE]
                parts.append(jnp.where((part == thr_b) & (idx > cm_b), thr_b - 1, part))
            key_ref[kblk] = _tile_lanes(parts[0], 1) if nrep == 1 else jnp.concatenate(parts, axis=1)
            return carry

        lax.fori_loop(0, nkb, rewrite, 0)

    thr_kb = _tile_lanes(jnp.broadcast_to(thr, (tq, LANE)), nrep)
    m_ref[...] = jnp.full(m_ref.shape, NEG_INF, F32)
    l_ref[...] = jnp.zeros(l_ref.shape, F32)
    acc_ref[...] = jnp.zeros(acc_ref.shape, F32)
    group = H_C // H_KV

    def p3(kblk, carry):
        k0 = pl.multiple_of(kblk * kb, kb)
        sel = key_ref[kblk] >= thr_kb
        for g in range(H_KV):
            gs = slice(g * DH_C, (g + 1) * DH_C)
            kg = k_ref[0, pl.ds(k0, kb), gs]
            vg = v_ref[0, pl.ds(k0, kb), gs]
            for hq in range(group):
                h = g * group + hq
                s = lax.dot_general(q_ref[:, h * DH_C:(h + 1) * DH_C], kg, (((1,), (1,)), ((), ())),
                                    preferred_element_type=F32) * (DH_C ** -0.5)
                s = jnp.where(sel, s, NEG_INF)
                m_prev = m_ref[h]
                m_new = jnp.maximum(m_prev, jnp.max(s, axis=1, keepdims=True))
                m_safe = jnp.where(m_new == NEG_INF, 0.0, m_new)
                alpha = jnp.exp(m_prev - m_safe)
                p = jnp.exp(s - _tile_lanes(m_safe, nrep))
                l_ref[h] = alpha * l_ref[h] + jnp.sum(p, axis=1, keepdims=True)
                acc_ref[h] = alpha * acc_ref[h] + jnp.dot(p.astype(BF16), vg, preferred_element_type=F32)
                m_ref[h] = m_new
        return carry

    lax.fori_loop(0, nkb, p3, 0)

    for h in range(H_C):
        hs = slice(h * DH_C, (h + 1) * DH_C)
        o = acc_ref[h] / l_ref[h]
        o_ref[:, hs] = (o * _silu(z_ref[:, hs])).astype(BF16)


def _dsa(proj, qi_r, q_r, kit, k_all, v_all, bsz, t_len, n_valid, pos0, *, tq, kb):
    n_keys = k_all.shape[1]
    nq = t_len // tq
    rows = bsz * t_len
    topk = min(TOPK, n_valid // 4)
    kern = functools.partial(_dsa_kernel, tq=tq, kb=kb, n_keys=n_keys, n_valid=n_valid, pos0=pos0, topk=topk)
    return pl.pallas_call(
        kern,
        grid=(bsz, nq),
        in_specs=[
            pl.BlockSpec((tq, 512), lambda b, i: (b * nq + i, 0)),
            pl.BlockSpec((tq, LANE), lambda b, i: (b * nq + i, COL_SMALL // LANE)),
            pl.BlockSpec((1, n_keys // kb, D_I, kb), lambda b, i: (b, 0, 0, 0)),
            pl.BlockSpec((tq, 1024), lambda b, i: (b * nq + i, 0)),
            pl.BlockSpec((1, n_keys, 256), lambda b, i: (b, 0, 0)),
            pl.BlockSpec((1, n_keys, 256), lambda b, i: (b, 0, 0)),
            pl.BlockSpec((tq, 1024), lambda b, i: (b * nq + i, COL_ZC // 1024)),
        ],
        out_specs=pl.BlockSpec((tq, 1024), lambda b, i: (b * nq + i, 0)),
        out_shape=jax.ShapeDtypeStruct((rows, 1024), BF16),
        scratch_shapes=[
            pltpu.VMEM((n_keys // kb, tq, kb), I32),
            pltpu.VMEM((H_I, tq, LANE), F32),
            pltpu.VMEM((H_C, tq, LANE), F32),
            pltpu.VMEM((H_C, tq, LANE), F32),
            pltpu.VMEM((H_C, tq, DH_C), F32),
        ],
        compiler_params=_cparams(("parallel", "arbitrary")),
        name="dsa",
    )(qi_r, proj, kit, q_r, k_all, v_all, proj)


def _merge_kernel(oa_ref, ub_ref, oc_ref, gate_ref, x_ref, woa_ref, wpw_ref, woc_ref, wout_ref, fg_ref, xo_ref,
                  *rest, final):
    ya = jnp.dot(oa_ref[...], woa_ref[0], preferred_element_type=F32)
    mix = _sigmoid(gate_ref[:, 0:D_MODEL]) * ya
    yb = jnp.dot(ub_ref[...], wpw_ref[0], preferred_element_type=F32)
    mix = mix + _sigmoid(gate_ref[:, D_MODEL:2 * D_MODEL]) * yb
    yc = jnp.dot(oc_ref[...], woc_ref[0], preferred_element_type=F32)
    mix = mix + _sigmoid(gate_ref[:, 2 * D_MODEL:3 * D_MODEL]) * yc
    xn = x_ref[...] + jnp.dot(mix.astype(BF16), wout_ref[0], preferred_element_type=F32)
    xo_ref[...] = xn
    if final:
        ms = jnp.mean(xn * xn, axis=-1, keepdims=True)
        rest[0][...] = xn * lax.rsqrt(ms + EPS) * fg_ref[...]


def _merge(o_a, u_b, o_c, proj, x2d, w_oa, w_pw, w_oc, w_out, final_g, layer, final):
    rows = x2d.shape[0]
    tm = _pick(rows, (512, 256, 128, 64, 32, 16, 8))
    act = pl.BlockSpec((tm, D_MODEL), lambda i: (i, 0))
    wsp = pl.BlockSpec((1, D_MODEL, D_MODEL), lambda i: (layer, 0, 0))
    out_specs = [act]
    out_shape = [jax.ShapeDtypeStruct((rows, D_MODEL), F32)]
    if final:
        out_specs.append(act)
        out_shape.append(jax.ShapeDtypeStruct((rows, D_MODEL), F32))
    res = pl.pallas_call(
        functools.partial(_merge_kernel, final=final),
        grid=(rows // tm,),
        in_specs=[act, act, act, pl.BlockSpec((tm, 3 * D_MODEL), lambda i: (i, COL_GATE // (3 * D_MODEL))), act,
                  wsp, wsp, wsp, wsp, pl.BlockSpec((1, D_MODEL), lambda i: (0, 0))],
        out_specs=out_specs,
        out_shape=out_shape,
        compiler_params=_cparams(("parallel",)),
        name="merge",
    )(o_a, u_b, o_c, proj, x2d, w_oa, w_pw, w_oc, w_out, final_g)
    return res if final else (res[0], None)


def _permute_w_in(w_in):
    offs = [0]
    for s in IN_SIZES:
        offs.append(offs[-1] + s)
    (qkv, a_in, b_in, z_a, glu, z_b, q_c, k_c, v_c, qi, ki, wi, z_c, gate) = [
        w_in[..., offs[n]:offs[n + 1]] for n in range(len(IN_SIZES))]
    pad = jnp.zeros(w_in.shape[:-1] + (SM_KI - SM_WI - H_I,), w_in.dtype)
    small = jnp.concatenate([a_in, b_in, wi, pad, ki], axis=-1)
    w = jnp.concatenate([qkv, gate, glu, z_a, z_b, q_c, z_c, qi, k_c, v_c, small], axis=-1)
    assert w.shape[-1] == NP
    return w.astype(BF16)


def _stream_layer(x2d, layer, final, cfg, state, prm):
    bsz, t_len, pos0 = cfg["bsz"], cfg["t_len"], cfg["pos0"]
    proj = _inproj(x2d, prm["norm_g"], prm["w_in"], layer)

    o_a, delta_new = _delta(proj, bsz, t_len, state["conv_a8"], state["delta"], prm["conv_a_w"][layer],
                            prm["a_log"][layer], prm["dt_bias"][layer], prm["onorm_a_g"][layer][None, :],
                            tb=cfg["delta_tb"], hb=cfg["delta_hb"])
    qkv3 = proj[:, COL_QKV:COL_QKV + 3 * D_MODEL].reshape(bsz, t_len, 3 * D_MODEL)
    conv_a_new = qkv3[:, t_len - (SHORT_CONV - 1):, :]

    u_b, tail_b = _convb(proj, bsz, t_len, state["conv_b32"], prm["conv_b_w32"][layer],
                         prm["conv_b_bias"][layer][None, :], prm["ln_b_g"][layer][None, :],
                         prm["ln_b_b"][layer][None, :], tb=cfg["convb_tb"])
    conv_b_new = tail_b[:, HALO - (CONV_B - 1):, :]

    q_r, k_r, qi_r, ki_r = _rope(proj, bsz, t_len, pos0)
    v_new = proj[:, COL_VC:COL_VC + 256].reshape(bsz, t_len, 256)
    n_valid = pos0 + t_len
    kb = cfg["dsa_kb"]
    n_keys = -(-n_valid // kb) * kb
    parts_k, parts_v, parts_ki = [k_r], [v_new], [ki_r]
    if pos0:
        parts_k.insert(0, state["past_k"])
        parts_v.insert(0, state["past_v"])
        parts_ki.insert(0, state["past_ki"])
    if n_keys > n_valid:
        parts_k.append(jnp.zeros((bsz, n_keys - n_valid, 256), F32))
        parts_v.append(jnp.zeros((bsz, n_keys - n_valid, 256), F32))
        parts_ki.append(jnp.zeros((bsz, n_keys - n_valid, D_I), F32))
    k_all = jnp.concatenate(parts_k, axis=1).astype(BF16) if len(parts_k) > 1 else k_r.astype(BF16)
    v_all = jnp.concatenate(parts_v, axis=1).astype(BF16) if len(parts_v) > 1 else v_new.astype(BF16)
    ki_all = jnp.concatenate(parts_ki, axis=1) if len(parts_ki) > 1 else ki_r
    kit = ki_all.astype(BF16).reshape(bsz, n_keys // kb, kb, D_I).transpose(0, 1, 3, 2)
    o_c = _dsa(proj, qi_r, q_r, kit, k_all, v_all, bsz, t_len, n_valid, pos0, tq=cfg["dsa_tq"], kb=kb)

    x_new, y_fin = _merge(o_a, u_b, o_c, proj, x2d, prm["w_o_a"], prm["w_pw2_b"], prm["w_o_c"], prm["w_out"],
                          prm["final_norm_g"], layer, final)
    new_state = (k_r.reshape(bsz, t_len, H_KV, DH_C), v_new.reshape(bsz, t_len, H_KV, DH_C), ki_r, conv_a_new,
                 delta_new, conv_b_new)
    return x_new, y_fin, new_state


def _run(x_prompt, x_sample, cache_k, cache_v, cache_idx_k, state_conv_a, state_delta, state_conv_b, prm, cfg_p,
         cfg_s):
    n_layers = prm["norm_g"].shape[0]
    bp, tp = cfg_p["bsz"], cfg_p["t_len"]
    bs, ts = cfg_s["bsz"], cfg_s["t_len"]
    past = cfg_s["pos0"]
    xp = x_prompt.reshape(bp * tp, D_MODEL)
    xs = x_sample.reshape(bs * ts, D_MODEL)
    zero_state = {
        "conv_a8": jnp.zeros((bp, 8, 3 * D_MODEL), F32),
        "delta": jnp.zeros((bp, H_A, DK_A, DK_A), F32),
        "conv_b32": jnp.zeros((bp, HALO, C_B), F32),
    }
    st_p, st_s = [], []
    yp = ys = None
    for layer in range(n_layers):
        final = layer == n_layers - 1
        xp, yp, sp = _stream_layer(xp, layer, final, cfg_p, zero_state, prm)
        s_state = {
            "conv_a8": jnp.pad(state_conv_a[layer], ((0, 0), (8 - (SHORT_CONV - 1), 0), (0, 0))),
            "delta": state_delta[layer],
            "conv_b32": jnp.pad(state_conv_b[layer], ((0, 0), (HALO - (CONV_B - 1), 0), (0, 0))),
            "past_k": cache_k[layer].reshape(bs, past, 256),
            "past_v": cache_v[layer].reshape(bs, past, 256),
            "past_ki": cache_idx_k[layer],
        }
        xs, ys, ss = _stream_layer(xs, layer, final, cfg_s, s_state, prm)
        st_p.append(sp)
        st_s.append(ss)

    def stack(states, n):
        return jnp.stack([s[n] for s in states], axis=0)

    return ((yp.reshape(bp, tp, D_MODEL), ys.reshape(bs, ts, D_MODEL))
            + tuple(stack(st_p, n) for n in range(6)) + tuple(stack(st_s, n) for n in range(6)))


def _prep_params(norm_g, w_in, conv_a_w, a_log, dt_bias, onorm_a_g, w_o_a, conv_b_w, conv_b_bias, ln_b_g, ln_b_b,
                 w_pw2_b, w_o_c, w_out, final_norm_g):
    return {
        "norm_g": norm_g[:, None, :], "w_in": _permute_w_in(w_in), "conv_a_w": conv_a_w, "a_log": a_log, "dt_bias": dt_bias,
        "onorm_a_g": onorm_a_g, "w_o_a": w_o_a.astype(BF16),
        "conv_b_w32": jnp.pad(conv_b_w, ((0, 0), (0, HALO - CONV_B), (0, 0))),
        "conv_b_bias": conv_b_bias, "ln_b_g": ln_b_g, "ln_b_b": ln_b_b, "w_pw2_b": w_pw2_b.astype(BF16),
        "w_o_c": w_o_c.astype(BF16), "w_out": w_out.astype(BF16), "final_norm_g": final_norm_g[None, :],
    }


def _stream_cfg(bsz, t_len, pos0):
    c = CHUNK if t_len % CHUNK == 0 else t_len
    n_valid = pos0 + t_len
    if t_len % 256 == 0:
        tq, kb = 256, 512
    else:
        tq = t_len
        kb = 384 if n_valid > 384 else LANE
    delta_tb = _pick(t_len, (512, 256, 128, 64, t_len))
    if delta_tb % c:
        delta_tb = t_len
    return {
        "bsz": bsz, "t_len": t_len, "pos0": pos0,
        "delta_tb": delta_tb, "delta_hb": 4 if t_len >= 512 else H_A,
        "convb_tb": _pick(t_len, (256, 128, 64, 32)),
        "dsa_tq": tq, "dsa_kb": kb,
    }


def kernel(x_prompt, x_sample, cache_k, cache_v, cache_idx_k, state_conv_a, state_delta, state_conv_b, norm_g, w_in,
           conv_a_w, a_log, dt_bias, onorm_a_g, w_o_a, conv_b_w, conv_b_bias, ln_b_g, ln_b_b, w_pw2_b, w_o_c, w_out,
           final_norm_g):
    prm = _prep_params(norm_g, w_in, conv_a_w, a_log, dt_bias, onorm_a_g, w_o_a, conv_b_w, conv_b_bias, ln_b_g,
                       ln_b_b, w_pw2_b, w_o_c, w_out, final_norm_g)
    cfg_p = _stream_cfg(x_prompt.shape[0], x_prompt.shape[1], 0)
    cfg_s = _stream_cfg(x_sample.shape[0], x_sample.shape[1], cache_k.shape[2])
    return _run(x_prompt, x_sample, cache_k, cache_v, cache_idx_k, state_conv_a, state_delta, state_conv_b, prm,
                cfg_p, cfg_s)
```

```python
import functools
import math

import jax
import jax.numpy as jnp
from jax import lax
from jax.experimental import pallas as pl
from jax.experimental.pallas import tpu as pltpu

F32 = jnp.float32
BF16 = jnp.bfloat16
I32 = jnp.int32

D_MODEL = 1024
N_LAYERS = 4
CHUNK = 64
H_A = 8
DK_A = 128
SHORT_CONV = 4
C_B = 1024
CONV_B = 31
H_C = 8
DH_C = 128
H_KV = 2
H_I = 8
D_I = 64
TOPK = 256
ROPE_THETA = 500000.0
ROPE_FRACTION = 4
EPS = 1e-6
IN_SIZES = (3 * D_MODEL, H_A, H_A, D_MODEL, 2 * C_B, C_B, H_C * DH_C, H_KV * DH_C, H_KV * DH_C, H_I * D_I, D_I, H_I,
            H_C * DH_C, 3 * D_MODEL)

LANE = 128
HALO = 32

COL_QKV = 0
COL_GATE = 3072
COL_GLU = 6144
COL_ZA = 8192
COL_ZB = 9216
COL_QC = 10240
COL_ZC = 11264
COL_QI = 12288
COL_KC = 12800
COL_VC = 13056
COL_SMALL = 13312
NP = 13440
SM_A, SM_B, SM_WI, SM_KI = 0, 8, 16, 64

NEG_INF = float("-inf")
INT_MIN = -(2 ** 31)
NEG_INF_KEY = -2139095041

VMEM_LIMIT = 56 * 1024 * 1024


def _cparams(sem):
    return pltpu.CompilerParams(dimension_semantics=sem, vmem_limit_bytes=VMEM_LIMIT)


def _pick(n, cands):
    for c in cands:
        if n % c == 0:
            return c
    raise ValueError(f"no block size in {cands} divides {n}")


def _sigmoid(x):
    return jax.nn.sigmoid(x)


def _silu(x):
    return x * jax.nn.sigmoid(x)


def _inproj_kernel(x_ref, g_ref, w_ref, o_ref, xn_ref):
    @pl.when(pl.program_id(1) == 0)
    def _():
        x = x_ref[...]
        ms = jnp.mean(x * x, axis=-1, keepdims=True)
        xn_ref[...] = (x * lax.rsqrt(ms + EPS) * g_ref[0]).astype(BF16)

    o_ref[...] = jnp.dot(xn_ref[...], w_ref[0], preferred_element_type=F32)


def _inproj(x2d, g_all, w_all, layer):
    rows = x2d.shape[0]
    tm = _pick(rows, (512, 256, 128, 64, 32, 16, 8))
    tn = 1920
    return pl.pallas_call(
        _inproj_kernel,
        grid=(rows // tm, NP // tn),
        in_specs=[
            pl.BlockSpec((tm, D_MODEL), lambda i, j: (i, 0)),
            pl.BlockSpec((1, 1, D_MODEL), lambda i, j: (layer, 0, 0)),
            pl.BlockSpec((1, D_MODEL, tn), lambda i, j: (layer, 0, j)),
        ],
        out_specs=pl.BlockSpec((tm, tn), lambda i, j: (i, j)),
        out_shape=jax.ShapeDtypeStruct((rows, NP), F32),
        scratch_shapes=[pltpu.VMEM((tm, D_MODEL), BF16)],
        compiler_params=_cparams(("parallel", "arbitrary")),
        name="inproj",
    )(x2d, g_all, w_all)


def _rope_kernel(q_ref, k_ref, qi_ref, sm_ref, cq_ref, sq_ref, ci_ref, si_ref, qo_ref, ko_ref, qio_ref, kio_ref):
    tb = q_ref.shape[0]
    lane = lax.broadcasted_iota(I32, (tb, LANE), 1)
    l64 = lane & 63
    cq, sq, ci, si = cq_ref[...], sq_ref[...], ci_ref[...], si_ref[...]

    def rot_qk(x):
        sw = jnp.where(lane < 16, pltpu.roll(x, LANE - 16, 1), pltpu.roll(x, 16, 1))
        return jnp.where(lane < 32, x * cq + sw * sq, x)

    def rot_idx(x):
        sw = jnp.where(l64 < 8, pltpu.roll(x, LANE - 8, 1), pltpu.roll(x, 8, 1))
        return jnp.where(l64 < 16, x * ci + sw * si, x)

    for h in range(H_C):
        sl = slice(h * LANE, (h + 1) * LANE)
        qo_ref[:, sl] = rot_qk(q_ref[:, sl]).astype(BF16)
    for h in range(H_KV):
        sl = slice(h * LANE, (h + 1) * LANE)
        ko_ref[0, :, sl] = rot_qk(k_ref[:, sl])
    for h in range(H_I * D_I // LANE):
        sl = slice(h * LANE, (h + 1) * LANE)
        qio_ref[:, sl] = (rot_idx(qi_ref[:, sl]) * (D_I ** -0.5)).astype(BF16)
    kio_ref[0] = rot_idx(sm_ref[...])[:, SM_KI:SM_KI + D_I]


def _rope_tables(t_len, pos0):
    pos = (pos0 + jnp.arange(t_len)).astype(F32)

    def table(d):
        rot = d // ROPE_FRACTION
        half = rot // 2
        inv = ROPE_THETA ** (-jnp.arange(half, dtype=F32) * 2.0 / rot)
        ang = pos[:, None] * inv[None, :]
        cos, sin = jnp.cos(ang), jnp.sin(ang)
        c = jnp.concatenate([cos, cos, jnp.ones((t_len, d - rot), F32)], axis=1)
        s = jnp.concatenate([-sin, sin, jnp.zeros((t_len, d - rot), F32)], axis=1)
        return jnp.tile(c, (1, LANE // d)), jnp.tile(s, (1, LANE // d))

    cq, sq = table(DH_C)
    ci, si = table(D_I)
    return cq, sq, ci, si


def _rope(proj, bsz, t_len, pos0):
    tb = _pick(t_len, (512, 256, 128, 64, 32))
    nb = t_len // tb
    cq, sq, ci, si = _rope_tables(t_len, pos0)
    rows = bsz * t_len
    tab = pl.BlockSpec((tb, LANE), lambda b, j: (j, 0))
    return pl.pallas_call(
        _rope_kernel,
        grid=(bsz, nb),
        in_specs=[
            pl.BlockSpec((tb, 1024), lambda b, j: (b * nb + j, COL_QC // 1024)),
            pl.BlockSpec((tb, 256), lambda b, j: (b * nb + j, COL_KC // 256)),
            pl.BlockSpec((tb, 512), lambda b, j: (b * nb + j, COL_QI // 512)),
            pl.BlockSpec((tb, LANE), lambda b, j: (b * nb + j, COL_SMALL // LANE)),
            tab, tab, tab, tab,
        ],
        out_specs=[
            pl.BlockSpec((tb, 1024), lambda b, j: (b * nb + j, 0)),
            pl.BlockSpec((1, tb, 256), lambda b, j: (b, j, 0)),
            pl.BlockSpec((tb, 512), lambda b, j: (b * nb + j, 0)),
            pl.BlockSpec((1, tb, D_I), lambda b, j: (b, j, 0)),
        ],
        out_shape=[
            jax.ShapeDtypeStruct((rows, 1024), BF16),
            jax.ShapeDtypeStruct((bsz, t_len, 256), F32),
            jax.ShapeDtypeStruct((rows, 512), BF16),
            jax.ShapeDtypeStruct((bsz, t_len, D_I), F32),
        ],
        compiler_params=_cparams(("parallel", "parallel")),
        name="rope",
    )(proj, proj, proj, proj, cq, sq, ci, si)


def _mm(a, b):
    return jnp.dot(a.astype(BF16), b.astype(BF16), preferred_element_type=F32)


def _mm_nt(a, b):
    return lax.dot_general(a.astype(BF16), b.astype(BF16), (((1,), (1,)), ((), ())), preferred_element_type=F32)


def _mm_tn(a, b):
    return lax.dot_general(a.astype(BF16), b.astype(BF16), (((0,), (0,)), ((), ())), preferred_element_type=F32)


def _split(x):
    hi = x.astype(BF16)
    return hi, (x - hi.astype(F32)).astype(BF16)


def _mm3(a_hi, a_lo, b_hi, b_lo):
    def d(x, y):
        return jnp.dot(x, y, preferred_element_type=F32)
    return d(a_hi, b_hi) + (d(a_hi, b_lo) + d(a_lo, b_hi))


def _delta_kernel(alog_ref, dtb_ref, q_ref, k_ref, v_ref, z_ref, sm_ref, hq_ref, hk_ref, hv_ref, wq_ref, wk_ref,
                  wv_ref, og_ref, s0_ref, o_ref, sfin_ref, s_sc, xq_sc, xk_sc, xv_sc, u_sc, wq_sc, kd_sc,
                  qk_sc, ge_sc, *, c, tb, hb, cpi):
    hg = pl.program_id(1)
    tblk = pl.program_id(2)
    nc = tb // c
    n_st = int(math.log2(c))

    @pl.when(tblk == 0)
    def _():
        s_sc[...] = s0_ref[0]
        xq_sc[0:8, :] = hq_ref[0]
        xk_sc[0:8, :] = hk_ref[0]
        xv_sc[0:8, :] = hv_ref[0]

    @pl.when(tblk > 0)
    def _():
        xq_sc[0:8, :] = xq_sc[tb:tb + 8, :]
        xk_sc[0:8, :] = xk_sc[tb:tb + 8, :]
        xv_sc[0:8, :] = xv_sc[tb:tb + 8, :]

    xq_sc[8:8 + tb, :] = q_ref[...]
    xk_sc[8:8 + tb, :] = k_ref[...]
    xv_sc[8:8 + tb, :] = v_ref[...]

    lane = lax.broadcasted_iota(I32, (c, LANE), 1)
    row = lax.broadcasted_iota(I32, (c, LANE), 0)
    ri = lax.broadcasted_iota(I32, (c, c), 0)
    cj = lax.broadcasted_iota(I32, (c, c), 1)
    tri_incl = ri >= cj
    tri_strict = ri > cj
    eye = jnp.where(ri == cj, 1.0, 0.0).astype(F32)

    def cumsum_rows(x):
        s = 1
        while s < c:
            x = x + jnp.where(row >= s, pltpu.roll(x, s, 0), 0.0)
            s *= 2
        return x

    def conv_silu(x_sc, w_ref, r0, sl):
        win = x_sc[pl.ds(r0, c + 8), sl]
        acc = win[8:, :] * w_ref[SHORT_CONV - 1:SHORT_CONV, sl]
        for jj in range(SHORT_CONV - 1):
            sh = SHORT_CONV - 1 - jj
            acc = acc + pltpu.roll(win, sh, 0)[8:, :] * w_ref[jj:jj + 1, sl]
        return _silu(acc)

    def l2n(x):
        return x * lax.rsqrt(jnp.sum(x * x, axis=-1, keepdims=True) + EPS)

    def gates(ci, hh):
        h = hg * hb + hh
        sm = sm_ref[pl.ds(pl.multiple_of(ci * c, c), c), :]
        a_col = jnp.sum(jnp.where(lane == SM_A + h, sm, 0.0), axis=1, keepdims=True)
        b_col = jnp.sum(jnp.where(lane == SM_B + h, sm, 0.0), axis=1, keepdims=True)
        a_b = jnp.broadcast_to(a_col, (c, LANE)) + dtb_ref[h]
        softplus = jnp.maximum(a_b, 0.0) + jnp.log1p(jnp.exp(-jnp.abs(a_b)))
        g_b = -jnp.exp(jnp.full((c, LANE), alog_ref[h], F32)) * softplus
        return cumsum_rows(g_b), jnp.broadcast_to(_sigmoid(b_col), (c, LANE))

    def phase1(it, carry):
        items = [(it * cpi + cc, hh) for cc in range(cpi) for hh in range(hb)]
        st = []
        for ci, hh in items:
            r0 = pl.multiple_of(ci * c, c)
            sl = slice(hh * LANE, (hh + 1) * LANE)
            q = l2n(conv_silu(xq_sc, wq_ref, r0, sl)) * (DK_A ** -0.5)
            k = l2n(conv_silu(xk_sc, wk_ref, r0, sl))
            v = conv_silu(xv_sc, wv_ref, r0, sl)
            big_g, beta_b = gates(ci, hh)
            g_row = jnp.sum(eye * big_g[:, :c], axis=0, keepdims=True)
            edec = jnp.exp(jnp.minimum(big_g[:, :c] - g_row, 0.0))
            kb = k * beta_b
            eg = jnp.exp(big_g)
            g_last = big_g[c - 1:c, :]
            kd_sc[hh, ci] = k * jnp.exp(g_last - big_g)
            ge_sc[hh, ci] = jnp.broadcast_to(jnp.exp(g_last), (8, LANE))
            st.append(dict(ci=ci, hh=hh, q=q, k=k, kb=kb, edec=edec, qe=q * eg,
                           rhs=jnp.concatenate([v * beta_b, kb * eg], axis=1)))
        for d in st:
            kq = _mm_nt(jnp.concatenate([d["kb"], d["q"]], axis=0), d["k"])
            qk_sc[d["hh"], d["ci"]] = kq[c:] * jnp.where(tri_incl, d["edec"], 0.0)
            m = -(kq[:c] * jnp.where(tri_strict, d["edec"], 0.0))
            d["x"] = jnp.concatenate([m, eye], axis=0)
        for _ in range(n_st):
            for d in st:
                x_hi, x_lo = _split(d["x"])
                y = _mm3(x_hi, x_lo, x_hi[:c], x_lo[:c])
                d["x"] = jnp.concatenate([y[:c], d["x"][c:] + y[c:]], axis=0)
        for d in st:
            p_hi, p_lo = _split(d["x"][c:])
            r_hi, r_lo = _split(d["rhs"])
            uw = _mm3(p_hi, p_lo, r_hi, r_lo)
            u_sc[d["hh"], d["ci"]] = uw[:, :LANE]
            wq_sc[d["hh"], d["ci"]] = jnp.concatenate([uw[:, LANE:], d["qe"]], axis=0)
        return carry

    lax.fori_loop(0, nc // cpi, phase1, 0)

    og = og_ref[...]

    def phase2(ci, carry):
        r0 = pl.multiple_of(ci * c, c)
        s = [s_sc[hh] for hh in range(hb)]
        ws_qs = [_mm(wq_sc[hh, ci], s[hh]) for hh in range(hb)]
        v_new = [u_sc[hh, ci] - ws_qs[hh][:c] for hh in range(hb)]
        for hh in range(hb):
            s_sc[hh] = ge_sc[hh, ci][0:1, :] * s[hh] + _mm_tn(kd_sc[hh, ci], v_new[hh])
        for hh in range(hb):
            sl = slice(hh * LANE, (hh + 1) * LANE)
            o = ws_qs[hh][c:] + _mm(qk_sc[hh, ci], v_new[hh])
            o = o * lax.rsqrt(jnp.mean(o * o, axis=-1, keepdims=True) + EPS) * og
            o_ref[pl.ds(r0, c), sl] = (o * _silu(z_ref[pl.ds(r0, c), sl])).astype(BF16)
        return carry

    lax.fori_loop(0, nc, phase2, 0)
    sfin_ref[0] = s_sc[...]


def _delta(proj, bsz, t_len, hist8, s0, conv_w, a_log, dt_bias, onorm_g, *, tb, hb):
    c = CHUNK if t_len % CHUNK == 0 else t_len
    nb = t_len // tb
    hw = hb * LANE
    rows = bsz * t_len

    def col(off):
        return lambda b, g, t: (b * nb + t, off // hw + g)

    def hcol(off):
        return lambda b, g, t: (b, 0, off // hw + g)

    def wcol(off):
        return lambda b, g, t: (0, off // hw + g)

    smem = pl.BlockSpec(memory_space=pltpu.SMEM)
    nc = tb // c
    kern = functools.partial(_delta_kernel, c=c, tb=tb, hb=hb, cpi=2 if nc % 2 == 0 else 1)
    return pl.pallas_call(
        kern,
        grid=(bsz, H_A // hb, nb),
        in_specs=[
            smem, smem,
            pl.BlockSpec((tb, hw), col(COL_QKV)),
            pl.BlockSpec((tb, hw), col(COL_QKV + 1024)),
            pl.BlockSpec((tb, hw), col(COL_QKV + 2048)),
            pl.BlockSpec((tb, hw), col(COL_ZA)),
            pl.BlockSpec((tb, LANE), lambda b, g, t: (b * nb + t, COL_SMALL // LANE)),
            pl.BlockSpec((1, 8, hw), hcol(0)),
            pl.BlockSpec((1, 8, hw), hcol(1024)),
            pl.BlockSpec((1, 8, hw), hcol(2048)),
            pl.BlockSpec((SHORT_CONV, hw), wcol(0)),
            pl.BlockSpec((SHORT_CONV, hw), wcol(1024)),
            pl.BlockSpec((SHORT_CONV, hw), wcol(2048)),
            pl.BlockSpec((1, LANE), lambda b, g, t: (0, 0)),
            pl.BlockSpec((1, hb, DK_A, DK_A), lambda b, g, t: (b, g, 0, 0)),
        ],
        out_specs=[
            pl.BlockSpec((tb, hw), lambda b, g, t: (b * nb + t, g)),
            pl.BlockSpec((1, hb, DK_A, DK_A), lambda b, g, t: (b, g, 0, 0)),
        ],
        out_shape=[
            jax.ShapeDtypeStruct((rows, 1024), BF16),
            jax.ShapeDtypeStruct((bsz, H_A, DK_A, DK_A), F32),
        ],
        scratch_shapes=[
            pltpu.VMEM((hb, DK_A, DK_A), F32),
            pltpu.VMEM((tb + 8, hw), F32),
            pltpu.VMEM((tb + 8, hw), F32),
            pltpu.VMEM((tb + 8, hw), F32),
            pltpu.VMEM((hb, nc, c, LANE), F32),
            pltpu.VMEM((hb, nc, 2 * c, LANE), F32),
            pltpu.VMEM((hb, nc, c, LANE), F32),
            pltpu.VMEM((hb, nc, c, c), F32),
            pltpu.VMEM((hb, nc, 8, LANE), F32),
        ],
        compiler_params=_cparams(("parallel", "parallel", "arbitrary")),
        name="delta",
    )(a_log, dt_bias, proj, proj, proj, proj, proj, hist8, hist8, hist8, conv_w, conv_w, conv_w, onorm_g, s0)


def _convb_kernel(glu_ref, halo_ref, hist_ref, z_ref, w_ref, bias_ref, lng_ref, lnb_ref, o_ref, tail_ref, xp_ref,
                  y_ref, *, tb):
    j = pl.program_id(1)

    @pl.when(j == 0)
    def _():
        xp_ref[0:HALO, :] = hist_ref[0]

    @pl.when(j > 0)
    def _():
        hl = halo_ref[...]
        xp_ref[0:HALO, :] = hl[:, :C_B] * _sigmoid(hl[:, C_B:])

    gl = glu_ref[...]
    xp_ref[HALO:HALO + tb, :] = gl[:, :C_B] * _sigmoid(gl[:, C_B:])
    tail_ref[0] = xp_ref[tb:tb + HALO, :]

    rc_n = min(tb, 64)
    cc_n = 256
    base = HALO - (CONV_B - 1)
    for cc in range(C_B // cc_n):
        cs = slice(cc * cc_n, (cc + 1) * cc_n)
        for rc in range(tb // rc_n):
            acc = jnp.zeros((rc_n, cc_n), F32)
            for k in range(CONV_B):
                r = rc * rc_n + base + k
                acc = acc + w_ref[k:k + 1, cs] * xp_ref[r:r + rc_n, cs]
            y_ref[rc * rc_n:(rc + 1) * rc_n, cs] = acc

    y = y_ref[...] + bias_ref[...]
    mu = jnp.mean(y, axis=-1, keepdims=True)
    yc = y - mu
    var = jnp.mean(yc * yc, axis=-1, keepdims=True)
    yn = yc * lax.rsqrt(var + EPS) * lng_ref[...] + lnb_ref[...]
    o_ref[...] = (_silu(yn) * _silu(z_ref[...])).astype(BF16)


def _convb(proj, bsz, t_len, hist32, w32, bias, ln_g, ln_b, *, tb):
    nb = t_len // tb
    rows = bsz * t_len
    rb = tb // HALO
    vec = pl.BlockSpec((1, C_B), lambda b, j: (0, 0))
    kern = functools.partial(_convb_kernel, tb=tb)
    return pl.pallas_call(
        kern,
        grid=(bsz, nb),
        in_specs=[
            pl.BlockSpec((tb, 2 * C_B), lambda b, j: (b * nb + j, COL_GLU // (2 * C_B))),
            pl.BlockSpec((HALO, 2 * C_B), lambda b, j: (jnp.maximum((b * nb + j) * rb - 1, 0), COL_GLU // (2 * C_B))),
            pl.BlockSpec((1, HALO, C_B), lambda b, j: (b, 0, 0)),
            pl.BlockSpec((tb, C_B), lambda b, j: (b * nb + j, COL_ZB // C_B)),
            pl.BlockSpec((HALO, C_B), lambda b, j: (0, 0)),
            vec, vec, vec,
        ],
        out_specs=[
            pl.BlockSpec((tb, C_B), lambda b, j: (b * nb + j, 0)),
            pl.BlockSpec((1, HALO, C_B), lambda b, j: (b, 0, 0)),
        ],
        out_shape=[
            jax.ShapeDtypeStruct((rows, C_B), BF16),
            jax.ShapeDtypeStruct((bsz, HALO, C_B), F32),
        ],
        scratch_shapes=[pltpu.VMEM((HALO + tb, C_B), F32), pltpu.VMEM((tb, C_B), F32)],
        compiler_params=_cparams(("parallel", "arbitrary")),
        name="convb",
    )(proj, proj, hist32, proj, w32, bias, ln_g, ln_b)


def _tile_lanes(x, n):
    return x if n == 1 else jnp.concatenate([x] * n, axis=1)


def _dsa_kernel(qi_ref, sm_ref, kit_ref, q_ref, k_ref, v_ref, z_ref, o_ref, key_ref, wb_ref, m_ref, l_ref, acc_ref,
                *, tq, kb, n_keys, n_valid, pos0, topk):
    i = pl.program_id(1)
    nrep = kb // LANE
    last_pos = pos0 + (i + 1) * tq - 1
    lim_last = jnp.minimum((last_pos // CHUNK + 1) * CHUNK, n_valid)
    nkb = jnp.minimum((lim_last + kb - 1) // kb, n_keys // kb)
    qpos = pos0 + i * tq + lax.broadcasted_iota(I32, (tq, 1), 0)
    lim_row = jnp.minimum(((qpos >> 6) + 1) << 6, n_valid)
    lane_kb = lax.broadcasted_iota(I32, (1, kb), 1)
    lane_tq = lax.broadcasted_iota(I32, (tq, LANE), 1)

    sm = sm_ref[...]
    for j in range(H_I):
        wb_ref[j] = jnp.broadcast_to(sm[:, SM_WI + j:SM_WI + j + 1] * (H_I ** -0.5), (tq, LANE))

    def p1(kblk, carry):
        kt = kit_ref[0, kblk]
        acc = jnp.zeros((tq, kb), F32)
        for j in range(H_I):
            lg = jnp.dot(qi_ref[:, j * D_I:(j + 1) * D_I], kt, preferred_element_type=F32)
            acc = acc + _tile_lanes(wb_ref[j], nrep) * jnp.maximum(lg, 0.0)
        adm = (kblk * kb + lane_kb) < lim_row
        bits = lax.bitcast_convert_type(jnp.where(adm, acc, NEG_INF), I32)
        key_ref[kblk] = bits ^ ((bits >> 31) & 0x7FFFFFFF)
        return carry

    lax.fori_loop(0, nkb, p1, 0)

    def count_ge(cand):
        cb = jnp.broadcast_to(cand, (tq, LANE))

        def body(kblk, cnt):
            blk = key_ref[kblk]
            for t in range(nrep):
                cnt = cnt + jnp.where(blk[:, t * LANE:(t + 1) * LANE] >= cb, 1.0, 0.0)
            return cnt

        cnt = lax.fori_loop(0, nkb, body, jnp.zeros((tq, LANE), F32))
        return jnp.sum(cnt, axis=1, keepdims=True)

    def bisect(p, t_u):
        cand_u = t_u | jnp.left_shift(jnp.int32(1), 31 - p)
        cnt = count_ge(cand_u ^ INT_MIN)
        return jnp.where(cnt >= topk, cand_u, t_u)

    t_u = lax.fori_loop(0, 32, bisect, jnp.zeros((tq, 1), I32))
    thr = jnp.maximum(t_u ^ INT_MIN, NEG_INF_KEY + 1)
    cnt_ge = count_ge(thr)

    @pl.when(jnp.max(cnt_ge) > topk)
    def _():
        need = topk - count_ge(thr + 1)
        thr_b = jnp.broadcast_to(thr, (tq, LANE))

        def count_tie_lt(cidx):
            cb = jnp.broadcast_to(cidx, (tq, LANE))

            def body(kblk, cnt):
                blk = key_ref[kblk]
                for t in range(nrep):
                    idx = kblk * kb + t * LANE + lane_tq
                    hit = (blk[:, t * LANE:(t + 1) * LANE] == thr_b) & (idx < cb)
                    cnt = cnt + jnp.where(hit, 1.0, 0.0)
                return cnt

            cnt = lax.fori_loop(0, nkb, body, jnp.zeros((tq, LANE), F32))
            return jnp.sum(cnt, axis=1, keepdims=True)

        nbits = max(n_keys - 1, 1).bit_length()

        def bis_idx(p, cm):
            cand = cm | jnp.left_shift(jnp.int32(1), nbits - 1 - p)
            return jnp.where(count_tie_lt(cand) < need, cand, cm)

        cm = lax.fori_loop(0, nbits, bis_idx, jnp.zeros((tq, 1), I32))
        cm_b = jnp.broadcast_to(cm, (tq, LANE))

        def rewrite(kblk, carry):
            blk = key_ref[kblk]
            parts = []
            for t in range(nrep):
                idx = kblk * kb + t * LANE + lane_tq
                part = blk[:, t * LANE:(t + 1) * LANE]
                parts.append(jnp.where((part == thr_b) & (idx > cm_b), thr_b - 1, part))
            key_ref[kblk] = _tile_lanes(parts[0], 1) if nrep == 1 else jnp.concatenate(parts, axis=1)
            return carry

        lax.fori_loop(0, nkb, rewrite, 0)

    thr_kb = _tile_lanes(jnp.broadcast_to(thr, (tq, LANE)), nrep)
    m_ref[...] = jnp.full(m_ref.shape, NEG_INF, F32)
    l_ref[...] = jnp.zeros(l_ref.shape, F32)
    acc_ref[...] = jnp.zeros(acc_ref.shape, F32)
    group = H_C // H_KV

    def p3(kblk, carry):
        k0 = pl.multiple_of(kblk * kb, kb)
        sel = key_ref[kblk] >= thr_kb
        for g in range(H_KV):
            gs = slice(g * DH_C, (g + 1) * DH_C)
            kg = k_ref[0, pl.ds(k0, kb), gs]
            vg = v_ref[0, pl.ds(k0, kb), gs]
            for hq in range(group):
                h = g * group + hq
                s = lax.dot_general(q_ref[:, h * DH_C:(h + 1) * DH_C], kg, (((1,), (1,)), ((), ())),
                                    preferred_element_type=F32) * (DH_C ** -0.5)
                s = jnp.where(sel, s, NEG_INF)
                m_prev = m_ref[h]
                m_new = jnp.maximum(m_prev, jnp.max(s, axis=1, keepdims=True))
                m_safe = jnp.where(m_new == NEG_INF, 0.0, m_new)
                alpha = jnp.exp(m_prev - m_safe)
                p = jnp.exp(s - _tile_lanes(m_safe, nrep))
                l_ref[h] = alpha * l_ref[h] + jnp.sum(p, axis=1, keepdims=True)
                acc_ref[h] = alpha * acc_ref[h] + jnp.dot(p.astype(BF16), vg, preferred_element_type=F32)
                m_ref[h] = m_new
        return carry

    lax.fori_loop(0, nkb, p3, 0)

    for h in range(H_C):
        hs = slice(h * DH_C, (h + 1) * DH_C)
        o = acc_ref[h] / l_ref[h]
        o_ref[:, hs] = (o * _silu(z_ref[:, hs])).astype(BF16)


def _dsa(proj, qi_r, q_r, kit, k_all, v_all, bsz, t_len, n_valid, pos0, *, tq, kb):
    n_keys = k_all.shape[1]
    nq = t_len // tq
    rows = bsz * t_len
    topk = min(TOPK, n_valid // 4)
    kern = functools.partial(_dsa_kernel, tq=tq, kb=kb, n_keys=n_keys, n_valid=n_valid, pos0=pos0, topk=topk)
    return pl.pallas_call(
        kern,
        grid=(bsz, nq),
        in_specs=[
            pl.BlockSpec((tq, 512), lambda b, i: (b * nq + i, 0)),
            pl.BlockSpec((tq, LANE), lambda b, i: (b * nq + i, COL_SMALL // LANE)),
            pl.BlockSpec((1, n_keys // kb, D_I, kb), lambda b, i: (b, 0, 0, 0)),
            pl.BlockSpec((tq, 1024), lambda b, i: (b * nq + i, 0)),
            pl.BlockSpec((1, n_keys, 256), lambda b, i: (b, 0, 0)),
            pl.BlockSpec((1, n_keys, 256), lambda b, i: (b, 0, 0)),
            pl.BlockSpec((tq, 1024), lambda b, i: (b * nq + i, COL_ZC // 1024)),
        ],
        out_specs=pl.BlockSpec((tq, 1024), lambda b, i: (b * nq + i, 0)),
        out_shape=jax.ShapeDtypeStruct((rows, 1024), BF16),
        scratch_shapes=[
            pltpu.VMEM((n_keys // kb, tq, kb), I32),
            pltpu.VMEM((H_I, tq, LANE), F32),
            pltpu.VMEM((H_C, tq, LANE), F32),
            pltpu.VMEM((H_C, tq, LANE), F32),
            pltpu.VMEM((H_C, tq, DH_C), F32),
        ],
        compiler_params=_cparams(("parallel", "arbitrary")),
        name="dsa",
    )(qi_r, proj, kit, q_r, k_all, v_all, proj)


SUB = 128


def _tree(parts, op):
    parts = list(parts)
    while len(parts) > 1:
        nxt = [op(parts[n], parts[n + 1]) for n in range(0, len(parts) - 1, 2)]
        if len(parts) % 2:
            nxt.append(parts[-1])
        parts = nxt
    return parts[0]


def _fold8(x, op):
    return _tree([x[r:r + 8, :] for r in range(0, x.shape[0], 8)], op)


def _dsa_t_kernel(qi_ref, wt_ref, ki_ref, q_ref, k_ref, vt_ref, z_ref, o_ref, key_ref, bias_ref, s_ref, p_ref, m_ref,
                  l_ref, acc_ref, *, tq, kb, n_keys, n_valid, pos0, topk):
    i = pl.program_id(1)
    nsub = kb // SUB
    last_pos = pos0 + (i + 1) * tq - 1
    lim_last = jnp.minimum((last_pos // CHUNK + 1) * CHUNK, n_valid)
    nkb = jnp.minimum((lim_last + kb - 1) // kb, n_keys // kb)
    qpos = pos0 + i * tq + lax.broadcasted_iota(I32, (1, tq), 1)
    lim = jnp.minimum(((qpos >> 6) + 1) << 6, n_valid)
    sub_iota = lax.broadcasted_iota(I32, (SUB, 1), 0)
    wt = wt_ref[0] * (H_I ** -0.5)
    nt = (((1,), (1,)), ((), ()))

    def p1(kblk, carry):
        for sb in range(nsub):
            r0 = pl.multiple_of(kblk * kb + sb * SUB, SUB)
            kis = ki_ref[0, pl.ds(r0, SUB), :]
            acc = jnp.zeros((SUB, tq), F32)
            for j in range(H_I):
                lg = lax.dot_general(kis, qi_ref[:, j * D_I:(j + 1) * D_I], nt, preferred_element_type=F32)
                acc = acc + wt[j:j + 1, :] * jnp.maximum(lg, 0.0)
            bits = lax.bitcast_convert_type(jnp.where((r0 + sub_iota) < lim, acc, NEG_INF), I32)
            key_ref[kblk, sb * SUB:(sb + 1) * SUB, :] = bits ^ ((bits >> 31) & 0x7FFFFFFF)
        return carry

    lax.fori_loop(0, nkb, p1, 0)

    def count(pred):
        def body(kblk, cnt):
            parts = [cnt]
            for sb in range(nsub):
                blk = key_ref[kblk, sb * SUB:(sb + 1) * SUB, :]
                parts.append(_fold8(jnp.where(pred(blk, kblk * kb + sb * SUB), 1.0, 0.0), jnp.add))
            return _tree(parts, jnp.add)

        cnt = lax.fori_loop(0, nkb, body, jnp.zeros((8, tq), F32))
        return jnp.sum(cnt, axis=0, keepdims=True)

    def count_ge(cand):
        return count(lambda blk, _: blk >= cand)

    def bisect(p, t_u):
        cand_u = t_u | jnp.left_shift(jnp.int32(1), 31 - p)
        return jnp.where(count_ge(cand_u ^ INT_MIN) >= topk, cand_u, t_u)

    t_u = lax.fori_loop(0, 32, bisect, jnp.zeros((1, tq), I32))
    thr = jnp.maximum(t_u ^ INT_MIN, NEG_INF_KEY + 1)
    cnt_ge = count_ge(thr)

    @pl.when(jnp.max(cnt_ge) > topk)
    def _():
        need = topk - count_ge(thr + 1)
        nbits = max(n_keys - 1, 1).bit_length()

        def bis_idx(p, cm):
            cand = cm | jnp.left_shift(jnp.int32(1), nbits - 1 - p)
            below = count(lambda blk, k0: (blk == thr) & ((k0 + sub_iota) < cand))
            return jnp.where(below < need, cand, cm)

        cm = lax.fori_loop(0, nbits, bis_idx, jnp.zeros((1, tq), I32))

        def rewrite(kblk, carry):
            for sb in range(nsub):
                blk = key_ref[kblk, sb * SUB:(sb + 1) * SUB, :]
                drop = (blk == thr) & ((kblk * kb + sb * SUB + sub_iota) > cm)
                key_ref[kblk, sb * SUB:(sb + 1) * SUB, :] = jnp.where(drop, thr - 1, blk)
            return carry

        lax.fori_loop(0, nkb, rewrite, 0)

    m_ref[...] = jnp.full(m_ref.shape, NEG_INF, F32)
    l_ref[...] = jnp.zeros(l_ref.shape, F32)
    acc_ref[...] = jnp.zeros(acc_ref.shape, F32)
    group = H_C // H_KV
    c_exp = (DH_C ** -0.5) * math.log2(math.e)

    def p3(kblk, carry):
        k0 = pl.multiple_of(kblk * kb, kb)
        for sb in range(nsub):
            ss = slice(sb * SUB, (sb + 1) * SUB)
            bias_ref[ss, :] = jnp.where(key_ref[kblk, ss, :] >= thr, 0.0, NEG_INF)

        def logits(h):
            g = h // group
            mx = []
            for sb in range(nsub):
                ss = slice(sb * SUB, (sb + 1) * SUB)
                ks = k_ref[0, pl.ds(k0 + sb * SUB, SUB), g * DH_C:(g + 1) * DH_C]
                s = lax.dot_general(ks, q_ref[:, h * DH_C:(h + 1) * DH_C], nt, preferred_element_type=F32)
                s = s + bias_ref[ss, :]
                s_ref[h % 2, ss, :] = s
                mx.append(_fold8(s, jnp.maximum))
            return jnp.max(_tree(mx, jnp.maximum), axis=0, keepdims=True)

        m_cur = logits(0)
        for h in range(H_C):
            m_nxt = logits(h + 1) if h + 1 < H_C else None
            g = h // group
            m_prev = m_ref[h][0:1, :]
            m_new = jnp.maximum(m_prev, m_cur)
            m_safe = jnp.where(m_new == NEG_INF, 0.0, m_new)
            alpha = jnp.exp2((m_prev - m_safe) * c_exp)
            ls = []
            for sb in range(nsub):
                ss = slice(sb * SUB, (sb + 1) * SUB)
                p = jnp.exp2((s_ref[h % 2, ss, :] - m_safe) * c_exp)
                ls.append(_fold8(p, jnp.add))
                p_ref[ss, :] = p.astype(BF16)
            l_new = alpha * l_ref[h][0:1, :] + jnp.sum(_tree(ls, jnp.add), axis=0, keepdims=True)
            pv = jnp.dot(vt_ref[0, kblk, g * DH_C:(g + 1) * DH_C, :], p_ref[...], preferred_element_type=F32)
            acc_ref[h] = alpha * acc_ref[h] + pv
            l_ref[h] = jnp.broadcast_to(l_new, (8, tq))
            m_ref[h] = jnp.broadcast_to(m_new, (8, tq))
            m_cur = m_nxt
        return carry

    lax.fori_loop(0, nkb, p3, 0)

    for h in range(H_C):
        hs = slice(h * DH_C, (h + 1) * DH_C)
        o = (acc_ref[h] / l_ref[h][0:1, :]).T
        o_ref[:, hs] = (o * _silu(z_ref[:, hs])).astype(BF16)


def _dsa_t(proj, qi_r, q_r, wi_t, ki_all, k_all, vt, bsz, t_len, n_valid, pos0, *, tq, kb):
    n_keys = k_all.shape[1]
    nq = t_len // tq
    rows = bsz * t_len
    topk = min(TOPK, n_valid // 4)
    kern = functools.partial(_dsa_t_kernel, tq=tq, kb=kb, n_keys=n_keys, n_valid=n_valid, pos0=pos0, topk=topk)
    return pl.pallas_call(
        kern,
        grid=(bsz, nq),
        in_specs=[
            pl.BlockSpec((tq, 512), lambda b, i: (b * nq + i, 0)),
            pl.BlockSpec((1, H_I, tq), lambda b, i: (b * nq + i, 0, 0)),
            pl.BlockSpec((1, n_keys, D_I), lambda b, i: (b, 0, 0)),
            pl.BlockSpec((tq, 1024), lambda b, i: (b * nq + i, 0)),
            pl.BlockSpec((1, n_keys, 256), lambda b, i: (b, 0, 0)),
            pl.BlockSpec((1, n_keys // kb, 256, kb), lambda b, i: (b, 0, 0, 0)),
            pl.BlockSpec((tq, 1024), lambda b, i: (b * nq + i, COL_ZC // 1024)),
        ],
        out_specs=pl.BlockSpec((tq, 1024), lambda b, i: (b * nq + i, 0)),
        out_shape=jax.ShapeDtypeStruct((rows, 1024), BF16),
        scratch_shapes=[
            pltpu.VMEM((n_keys // kb, kb, tq), I32),
            pltpu.VMEM((kb, tq), F32),
            pltpu.VMEM((2, kb, tq), F32),
            pltpu.VMEM((kb, tq), BF16),
            pltpu.VMEM((H_C, 8, tq), F32),
            pltpu.VMEM((H_C, 8, tq), F32),
            pltpu.VMEM((H_C, DH_C, tq), F32),
        ],
        compiler_params=_cparams(("parallel", "arbitrary")),
        name="dsa_t",
    )(qi_r, wi_t, ki_all, q_r, k_all, vt, proj)


def _merge_kernel(oa_ref, ub_ref, oc_ref, gate_ref, x_ref, woa_ref, wpw_ref, woc_ref, wout_ref, fg_ref, xo_ref,
                  *rest, final):
    ya = jnp.dot(oa_ref[...], woa_ref[0], preferred_element_type=F32)
    mix = _sigmoid(gate_ref[:, 0:D_MODEL]) * ya
    yb = jnp.dot(ub_ref[...], wpw_ref[0], preferred_element_type=F32)
    mix = mix + _sigmoid(gate_ref[:, D_MODEL:2 * D_MODEL]) * yb
    yc = jnp.dot(oc_ref[...], woc_ref[0], preferred_element_type=F32)
    mix = mix + _sigmoid(gate_ref[:, 2 * D_MODEL:3 * D_MODEL]) * yc
    xn = x_ref[...] + jnp.dot(mix.astype(BF16), wout_ref[0], preferred_element_type=F32)
    xo_ref[...] = xn
    if final:
        ms = jnp.mean(xn * xn, axis=-1, keepdims=True)
        rest[0][...] = xn * lax.rsqrt(ms + EPS) * fg_ref[...]


def _merge(o_a, u_b, o_c, proj, x2d, w_oa, w_pw, w_oc, w_out, final_g, layer, final):
    rows = x2d.shape[0]
    tm = _pick(rows, (512, 256, 128, 64, 32, 16, 8))
    act = pl.BlockSpec((tm, D_MODEL), lambda i: (i, 0))
    wsp = pl.BlockSpec((1, D_MODEL, D_MODEL), lambda i: (layer, 0, 0))
    out_specs = [act]
    out_shape = [jax.ShapeDtypeStruct((rows, D_MODEL), F32)]
    if final:
        out_specs.append(act)
        out_shape.append(jax.ShapeDtypeStruct((rows, D_MODEL), F32))
    res = pl.pallas_call(
        functools.partial(_merge_kernel, final=final),
        grid=(rows // tm,),
        in_specs=[act, act, act, pl.BlockSpec((tm, 3 * D_MODEL), lambda i: (i, COL_GATE // (3 * D_MODEL))), act,
                  wsp, wsp, wsp, wsp, pl.BlockSpec((1, D_MODEL), lambda i: (0, 0))],
        out_specs=out_specs,
        out_shape=out_shape,
        compiler_params=_cparams(("parallel",)),
        name="merge",
    )(o_a, u_b, o_c, proj, x2d, w_oa, w_pw, w_oc, w_out, final_g)
    return res if final else (res[0], None)


def _permute_w_in(w_in):
    offs = [0]
    for s in IN_SIZES:
        offs.append(offs[-1] + s)
    (qkv, a_in, b_in, z_a, glu, z_b, q_c, k_c, v_c, qi, ki, wi, z_c, gate) = [
        w_in[..., offs[n]:offs[n + 1]] for n in range(len(IN_SIZES))]
    pad = jnp.zeros(w_in.shape[:-1] + (SM_KI - SM_WI - H_I,), w_in.dtype)
    small = jnp.concatenate([a_in, b_in, wi, pad, ki], axis=-1)
    w = jnp.concatenate([qkv, gate, glu, z_a, z_b, q_c, z_c, qi, k_c, v_c, small], axis=-1)
    assert w.shape[-1] == NP
    return w.astype(BF16)


def _stream_layer(x2d, layer, final, cfg, state, prm):
    bsz, t_len, pos0 = cfg["bsz"], cfg["t_len"], cfg["pos0"]
    proj = _inproj(x2d, prm["norm_g"], prm["w_in"], layer)

    o_a, delta_new = _delta(proj, bsz, t_len, state["conv_a8"], state["delta"], prm["conv_a_w"][layer],
                            prm["a_log"][layer], prm["dt_bias"][layer], prm["onorm_a_g"][layer][None, :],
                            tb=cfg["delta_tb"], hb=cfg["delta_hb"])
    qkv3 = proj[:, COL_QKV:COL_QKV + 3 * D_MODEL].reshape(bsz, t_len, 3 * D_MODEL)
    conv_a_new = qkv3[:, t_len - (SHORT_CONV - 1):, :]

    u_b, tail_b = _convb(proj, bsz, t_len, state["conv_b32"], prm["conv_b_w32"][layer],
                         prm["conv_b_bias"][layer][None, :], prm["ln_b_g"][layer][None, :],
                         prm["ln_b_b"][layer][None, :], tb=cfg["convb_tb"])
    conv_b_new = tail_b[:, HALO - (CONV_B - 1):, :]

    q_r, k_r, qi_r, ki_r = _rope(proj, bsz, t_len, pos0)
    v_new = proj[:, COL_VC:COL_VC + 256].reshape(bsz, t_len, 256)
    n_valid = pos0 + t_len
    kb = cfg["dsa_kb"]
    n_keys = -(-n_valid // kb) * kb
    parts_k, parts_v, parts_ki = [k_r], [v_new], [ki_r]
    if pos0:
        parts_k.insert(0, state["past_k"])
        parts_v.insert(0, state["past_v"])
        parts_ki.insert(0, state["past_ki"])
    if n_keys > n_valid:
        parts_k.append(jnp.zeros((bsz, n_keys - n_valid, 256), F32))
        parts_v.append(jnp.zeros((bsz, n_keys - n_valid, 256), F32))
        parts_ki.append(jnp.zeros((bsz, n_keys - n_valid, D_I), F32))
    k_all = jnp.concatenate(parts_k, axis=1).astype(BF16) if len(parts_k) > 1 else k_r.astype(BF16)
    v_all = jnp.concatenate(parts_v, axis=1).astype(BF16) if len(parts_v) > 1 else v_new.astype(BF16)
    ki_all = (jnp.concatenate(parts_ki, axis=1) if len(parts_ki) > 1 else ki_r).astype(BF16)
    tq = cfg["dsa_tq"]
    if tq % LANE == 0:
        wi_t = proj[:, COL_SMALL + SM_WI:COL_SMALL + SM_WI + H_I].reshape(bsz * t_len // tq, tq, H_I).transpose(0, 2, 1)
        vt = v_all.reshape(bsz, n_keys // kb, kb, 256).transpose(0, 1, 3, 2)
        o_c = _dsa_t(proj, qi_r, q_r, wi_t, ki_all, k_all, vt, bsz, t_len, n_valid, pos0, tq=tq, kb=kb)
    else:
        kit = ki_all.reshape(bsz, n_keys // kb, kb, D_I).transpose(0, 1, 3, 2)
        o_c = _dsa(proj, qi_r, q_r, kit, k_all, v_all, bsz, t_len, n_valid, pos0, tq=tq, kb=kb)

    x_new, y_fin = _merge(o_a, u_b, o_c, proj, x2d, prm["w_o_a"], prm["w_pw2_b"], prm["w_o_c"], prm["w_out"],
                          prm["final_norm_g"], layer, final)
    new_state = (k_r.reshape(bsz, t_len, H_KV, DH_C), v_new.reshape(bsz, t_len, H_KV, DH_C), ki_r, conv_a_new,
                 delta_new, conv_b_new)
    return x_new, y_fin, new_state


def _run(x_prompt, x_sample, cache_k, cache_v, cache_idx_k, state_conv_a, state_delta, state_conv_b, prm, cfg_p,
         cfg_s):
    n_layers = prm["norm_g"].shape[0]
    bp, tp = cfg_p["bsz"], cfg_p["t_len"]
    bs, ts = cfg_s["bsz"], cfg_s["t_len"]
    past = cfg_s["pos0"]
    xp = x_prompt.reshape(bp * tp, D_MODEL)
    xs = x_sample.reshape(bs * ts, D_MODEL)
    zero_state = {
        "conv_a8": jnp.zeros((bp, 8, 3 * D_MODEL), F32),
        "delta": jnp.zeros((bp, H_A, DK_A, DK_A), F32),
        "conv_b32": jnp.zeros((bp, HALO, C_B), F32),
    }
    st_p, st_s = [], []
    yp = ys = None
    for layer in range(n_layers):
        final = layer == n_layers - 1
        xp, yp, sp = _stream_layer(xp, layer, final, cfg_p, zero_state, prm)
        s_state = {
            "conv_a8": jnp.pad(state_conv_a[layer], ((0, 0), (8 - (SHORT_CONV - 1), 0), (0, 0))),
            "delta": state_delta[layer],
            "conv_b32": jnp.pad(state_conv_b[layer], ((0, 0), (HALO - (CONV_B - 1), 0), (0, 0))),
            "past_k": cache_k[layer].reshape(bs, past, 256),
            "past_v": cache_v[layer].reshape(bs, past, 256),
            "past_ki": cache_idx_k[layer],
        }
        xs, ys, ss = _stream_layer(xs, layer, final, cfg_s, s_state, prm)
        st_p.append(sp)
        st_s.append(ss)

    def stack(states, n):
        return jnp.stack([s[n] for s in states], axis=0)

    return ((yp.reshape(bp, tp, D_MODEL), ys.reshape(bs, ts, D_MODEL))
            + tuple(stack(st_p, n) for n in range(6)) + tuple(stack(st_s, n) for n in range(6)))


def _prep_params(norm_g, w_in, conv_a_w, a_log, dt_bias, onorm_a_g, w_o_a, conv_b_w, conv_b_bias, ln_b_g, ln_b_b,
                 w_pw2_b, w_o_c, w_out, final_norm_g):
    return {
        "norm_g": norm_g[:, None, :], "w_in": _permute_w_in(w_in), "conv_a_w": conv_a_w, "a_log": a_log, "dt_bias": dt_bias,
        "onorm_a_g": onorm_a_g, "w_o_a": w_o_a.astype(BF16),
        "conv_b_w32": jnp.pad(conv_b_w, ((0, 0), (0, HALO - CONV_B), (0, 0))),
        "conv_b_bias": conv_b_bias, "ln_b_g": ln_b_g, "ln_b_b": ln_b_b, "w_pw2_b": w_pw2_b.astype(BF16),
        "w_o_c": w_o_c.astype(BF16), "w_out": w_out.astype(BF16), "final_norm_g": final_norm_g[None, :],
    }


def _stream_cfg(bsz, t_len, pos0):
    c = CHUNK if t_len % CHUNK == 0 else t_len
    n_valid = pos0 + t_len
    if t_len % 256 == 0:
        tq, kb = 256, 512
    else:
        tq = t_len
        kb = 384 if n_valid > 384 else LANE
    delta_tb = _pick(t_len, (512, 256, 128, 64, t_len))
    if delta_tb % c:
        delta_tb = t_len
    return {
        "bsz": bsz, "t_len": t_len, "pos0": pos0,
        "delta_tb": delta_tb, "delta_hb": 4 if t_len >= 512 else H_A,
        "convb_tb": _pick(t_len, (256, 128, 64, 32)),
        "dsa_tq": tq, "dsa_kb": kb,
    }


def kernel(x_prompt, x_sample, cache_k, cache_v, cache_idx_k, state_conv_a, state_delta, state_conv_b, norm_g, w_in,
           conv_a_w, a_log, dt_bias, onorm_a_g, w_o_a, conv_b_w, conv_b_bias, ln_b_g, ln_b_b, w_pw2_b, w_o_c, w_out,
           final_norm_g):
    prm = _prep_params(norm_g, w_in, conv_a_w, a_log, dt_bias, onorm_a_g, w_o_a, conv_b_w, conv_b_bias, ln_b_g,
                       ln_b_b, w_pw2_b, w_o_c, w_out, final_norm_g)
    cfg_p = _stream_cfg(x_prompt.shape[0], x_prompt.shape[1], 0)
    cfg_s = _stream_cfg(x_sample.shape[0], x_sample.shape[1], cache_k.shape[2])
    return _run(x_prompt, x_sample, cache_k, cache_v, cache_idx_k, state_conv_a, state_delta, state_conv_b, prm,
                cfg_p, cfg_s)
```

```python
import functools
import math

import jax
import jax.numpy as jnp
from jax import lax
from jax.experimental import pallas as pl
from jax.experimental.pallas import tpu as pltpu

F32 = jnp.float32
BF16 = jnp.bfloat16
I32 = jnp.int32
I16 = jnp.int16

D_MODEL = 1024
N_LAYERS = 4
CHUNK = 64
H_A = 8
DK_A = 128
SHORT_CONV = 4
C_B = 1024
CONV_B = 31
H_C = 8
DH_C = 128
H_KV = 2
H_I = 8
D_I = 64
TOPK = 256
ROPE_THETA = 500000.0
ROPE_FRACTION = 4
EPS = 1e-6
IN_SIZES = (3 * D_MODEL, H_A, H_A, D_MODEL, 2 * C_B, C_B, H_C * DH_C, H_KV * DH_C, H_KV * DH_C, H_I * D_I, D_I, H_I,
            H_C * DH_C, 3 * D_MODEL)

LANE = 128
HALO = 32

COL_QKV = 0
COL_GATE = 3072
COL_GLU = 6144
COL_ZA = 8192
COL_ZB = 9216
COL_QC = 10240
COL_ZC = 11264
COL_QI = 12288
COL_KC = 12800
COL_VC = 13056
COL_SMALL = 13312
NP = 13440
SM_A, SM_B, SM_WI, SM_KI = 0, 8, 16, 64

NEG_INF = float("-inf")
INT_MIN = -(2 ** 31)
NEG_INF_KEY = -2139095041

VMEM_LIMIT = 56 * 1024 * 1024


def _cparams(sem):
    return pltpu.CompilerParams(dimension_semantics=sem, vmem_limit_bytes=VMEM_LIMIT)


def _pick(n, cands):
    for c in cands:
        if n % c == 0:
            return c
    raise ValueError(f"no block size in {cands} divides {n}")


def _sigmoid(x):
    return jax.nn.sigmoid(x)


def _silu(x):
    return x * jax.nn.sigmoid(x)


def _inproj_kernel(x_ref, g_ref, w_ref, o_ref, xn_ref):
    @pl.when(pl.program_id(1) == 0)
    def _():
        x = x_ref[...]
        ms = jnp.mean(x * x, axis=-1, keepdims=True)
        xn_ref[...] = (x * lax.rsqrt(ms + EPS) * g_ref[0]).astype(BF16)

    o_ref[...] = jnp.dot(xn_ref[...], w_ref[0], preferred_element_type=F32)


def _inproj(x2d, g_all, w_all, layer):
    rows = x2d.shape[0]
    tm = _pick(rows, (2048, 1024, 512, 256, 128, 64, 32, 16, 8))
    tn = 640
    return pl.pallas_call(
        _inproj_kernel,
        grid=(rows // tm, NP // tn),
        in_specs=[
            pl.BlockSpec((tm, D_MODEL), lambda i, j: (i, 0)),
            pl.BlockSpec((1, 1, D_MODEL), lambda i, j: (layer, 0, 0)),
            pl.BlockSpec((1, D_MODEL, tn), lambda i, j: (layer, 0, j)),
        ],
        out_specs=pl.BlockSpec((tm, tn), lambda i, j: (i, j)),
        out_shape=jax.ShapeDtypeStruct((rows, NP), F32),
        scratch_shapes=[pltpu.VMEM((tm, D_MODEL), BF16)],
        compiler_params=_cparams(("parallel", "arbitrary")),
        name="inproj",
    )(x2d, g_all, w_all)


def _rope_kernel(q_ref, k_ref, qi_ref, sm_ref, cq_ref, sq_ref, ci_ref, si_ref, qo_ref, ko_ref, qio_ref, kio_ref):
    tb = q_ref.shape[0]
    lane = lax.broadcasted_iota(I32, (tb, LANE), 1)
    l64 = lane & 63
    cq, sq, ci, si = cq_ref[...], sq_ref[...], ci_ref[...], si_ref[...]

    def rot_qk(x):
        sw = jnp.where(lane < 16, pltpu.roll(x, LANE - 16, 1), pltpu.roll(x, 16, 1))
        return jnp.where(lane < 32, x * cq + sw * sq, x)

    def rot_idx(x):
        sw = jnp.where(l64 < 8, pltpu.roll(x, LANE - 8, 1), pltpu.roll(x, 8, 1))
        return jnp.where(l64 < 16, x * ci + sw * si, x)

    for h in range(H_C):
        sl = slice(h * LANE, (h + 1) * LANE)
        qo_ref[:, sl] = rot_qk(q_ref[:, sl]).astype(BF16)
    for h in range(H_KV):
        sl = slice(h * LANE, (h + 1) * LANE)
        ko_ref[0, :, sl] = rot_qk(k_ref[:, sl])
    for h in range(H_I * D_I // LANE):
        sl = slice(h * LANE, (h + 1) * LANE)
        qio_ref[:, sl] = (rot_idx(qi_ref[:, sl]) * (D_I ** -0.5)).astype(BF16)
    kio_ref[0] = rot_idx(sm_ref[...])[:, SM_KI:SM_KI + D_I]


def _rope_tables(t_len, pos0):
    pos = (pos0 + jnp.arange(t_len)).astype(F32)

    def table(d):
        rot = d // ROPE_FRACTION
        half = rot // 2
        inv = ROPE_THETA ** (-jnp.arange(half, dtype=F32) * 2.0 / rot)
        ang = pos[:, None] * inv[None, :]
        cos, sin = jnp.cos(ang), jnp.sin(ang)
        c = jnp.concatenate([cos, cos, jnp.ones((t_len, d - rot), F32)], axis=1)
        s = jnp.concatenate([-sin, sin, jnp.zeros((t_len, d - rot), F32)], axis=1)
        return jnp.tile(c, (1, LANE // d)), jnp.tile(s, (1, LANE // d))

    cq, sq = table(DH_C)
    ci, si = table(D_I)
    return cq, sq, ci, si


def _rope(proj, bsz, t_len, pos0):
    tb = _pick(t_len, (512, 256, 128, 64, 32))
    nb = t_len // tb
    cq, sq, ci, si = _rope_tables(t_len, pos0)
    rows = bsz * t_len
    tab = pl.BlockSpec((tb, LANE), lambda b, j: (j, 0))
    return pl.pallas_call(
        _rope_kernel,
        grid=(bsz, nb),
        in_specs=[
            pl.BlockSpec((tb, 1024), lambda b, j: (b * nb + j, COL_QC // 1024)),
            pl.BlockSpec((tb, 256), lambda b, j: (b * nb + j, COL_KC // 256)),
            pl.BlockSpec((tb, 512), lambda b, j: (b * nb + j, COL_QI // 512)),
            pl.BlockSpec((tb, LANE), lambda b, j: (b * nb + j, COL_SMALL // LANE)),
            tab, tab, tab, tab,
        ],
        out_specs=[
            pl.BlockSpec((tb, 1024), lambda b, j: (b * nb + j, 0)),
            pl.BlockSpec((1, tb, 256), lambda b, j: (b, j, 0)),
            pl.BlockSpec((tb, 512), lambda b, j: (b * nb + j, 0)),
            pl.BlockSpec((1, tb, D_I), lambda b, j: (b, j, 0)),
        ],
        out_shape=[
            jax.ShapeDtypeStruct((rows, 1024), BF16),
            jax.ShapeDtypeStruct((bsz, t_len, 256), F32),
            jax.ShapeDtypeStruct((rows, 512), BF16),
            jax.ShapeDtypeStruct((bsz, t_len, D_I), F32),
        ],
        compiler_params=_cparams(("parallel", "parallel")),
        name="rope",
    )(proj, proj, proj, proj, cq, sq, ci, si)


def _mm(a, b):
    return jnp.dot(a.astype(BF16), b.astype(BF16), preferred_element_type=F32)


def _mm_nt(a, b):
    return lax.dot_general(a.astype(BF16), b.astype(BF16), (((1,), (1,)), ((), ())), preferred_element_type=F32)


def _mm_tn(a, b):
    return lax.dot_general(a.astype(BF16), b.astype(BF16), (((0,), (0,)), ((), ())), preferred_element_type=F32)


def _split(x):
    hi = x.astype(BF16)
    return hi, (x - hi.astype(F32)).astype(BF16)


def _mm3(a_hi, a_lo, b_hi, b_lo):
    def d(x, y):
        return jnp.dot(x, y, preferred_element_type=F32)
    return d(a_hi, b_hi) + (d(a_hi, b_lo) + d(a_lo, b_hi))


def _delta_kernel(alog_ref, dtb_ref, q_ref, k_ref, v_ref, z_ref, sm_ref, hq_ref, hk_ref, hv_ref, wq_ref, wk_ref,
                  wv_ref, og_ref, s0_ref, o_ref, sfin_ref, s_sc, xq_sc, xk_sc, xv_sc, u_sc, wq_sc, kd_sc,
                  qk_sc, ge_sc, *, c, tb, hb, cpi):
    hg = pl.program_id(1)
    tblk = pl.program_id(2)
    nc = tb // c
    n_st = int(math.log2(c))

    @pl.when(tblk == 0)
    def _():
        s_sc[...] = s0_ref[0]
        xq_sc[0:8, :] = hq_ref[0]
        xk_sc[0:8, :] = hk_ref[0]
        xv_sc[0:8, :] = hv_ref[0]

    @pl.when(tblk > 0)
    def _():
        xq_sc[0:8, :] = xq_sc[tb:tb + 8, :]
        xk_sc[0:8, :] = xk_sc[tb:tb + 8, :]
        xv_sc[0:8, :] = xv_sc[tb:tb + 8, :]

    xq_sc[8:8 + tb, :] = q_ref[...]
    xk_sc[8:8 + tb, :] = k_ref[...]
    xv_sc[8:8 + tb, :] = v_ref[...]

    lane = lax.broadcasted_iota(I32, (c, LANE), 1)
    row = lax.broadcasted_iota(I32, (c, LANE), 0)
    ri = lax.broadcasted_iota(I32, (c, c), 0)
    cj = lax.broadcasted_iota(I32, (c, c), 1)
    tri_incl = ri >= cj
    tri_strict = ri > cj
    eye = jnp.where(ri == cj, 1.0, 0.0).astype(F32)

    def cumsum_rows(x):
        s = 1
        while s < c:
            x = x + jnp.where(row >= s, pltpu.roll(x, s, 0), 0.0)
            s *= 2
        return x

    def conv_silu(x_sc, w_ref, r0, sl):
        win = x_sc[pl.ds(r0, c + 8), sl]
        acc = win[8:, :] * w_ref[SHORT_CONV - 1:SHORT_CONV, sl]
        for jj in range(SHORT_CONV - 1):
            sh = SHORT_CONV - 1 - jj
            acc = acc + pltpu.roll(win, sh, 0)[8:, :] * w_ref[jj:jj + 1, sl]
        return _silu(acc)

    def l2n(x):
        return x * lax.rsqrt(jnp.sum(x * x, axis=-1, keepdims=True) + EPS)

    def gates(ci, hh):
        h = hg * hb + hh
        sm = sm_ref[pl.ds(pl.multiple_of(ci * c, c), c), :]
        a_col = jnp.sum(jnp.where(lane == SM_A + h, sm, 0.0), axis=1, keepdims=True)
        b_col = jnp.sum(jnp.where(lane == SM_B + h, sm, 0.0), axis=1, keepdims=True)
        a_b = jnp.broadcast_to(a_col, (c, LANE)) + dtb_ref[h]
        softplus = jnp.maximum(a_b, 0.0) + jnp.log1p(jnp.exp(-jnp.abs(a_b)))
        g_b = -jnp.exp(jnp.full((c, LANE), alog_ref[h], F32)) * softplus
        return cumsum_rows(g_b), jnp.broadcast_to(_sigmoid(b_col), (c, LANE))

    def phase1(it, carry):
        items = [(it * cpi + cc, hh) for cc in range(cpi) for hh in range(hb)]
        st = []
        for ci, hh in items:
            r0 = pl.multiple_of(ci * c, c)
            sl = slice(hh * LANE, (hh + 1) * LANE)
            q = l2n(conv_silu(xq_sc, wq_ref, r0, sl)) * (DK_A ** -0.5)
            k = l2n(conv_silu(xk_sc, wk_ref, r0, sl))
            v = conv_silu(xv_sc, wv_ref, r0, sl)
            big_g, beta_b = gates(ci, hh)
            g_row = jnp.sum(eye * big_g[:, :c], axis=0, keepdims=True)
            edec = jnp.exp(jnp.minimum(big_g[:, :c] - g_row, 0.0))
            kb = k * beta_b
            eg = jnp.exp(big_g)
            g_last = big_g[c - 1:c, :]
            kd_sc[hh, ci] = k * jnp.exp(g_last - big_g)
            ge_sc[hh, ci] = jnp.broadcast_to(jnp.exp(g_last), (8, LANE))
            st.append(dict(ci=ci, hh=hh, q=q, k=k, kb=kb, edec=edec, qe=q * eg,
                           rhs=jnp.concatenate([v * beta_b, kb * eg], axis=1)))
        for d in st:
            kq = _mm_nt(jnp.concatenate([d["kb"], d["q"]], axis=0), d["k"])
            qk_sc[d["hh"], d["ci"]] = kq[c:] * jnp.where(tri_incl, d["edec"], 0.0)
            m = -(kq[:c] * jnp.where(tri_strict, d["edec"], 0.0))
            d["x"] = jnp.concatenate([m, eye], axis=0)
        for _ in range(n_st):
            for d in st:
                x_hi, x_lo = _split(d["x"])
                y = _mm3(x_hi, x_lo, x_hi[:c], x_lo[:c])
                d["x"] = jnp.concatenate([y[:c], d["x"][c:] + y[c:]], axis=0)
        for d in st:
            p_hi, p_lo = _split(d["x"][c:])
            r_hi, r_lo = _split(d["rhs"])
            uw = _mm3(p_hi, p_lo, r_hi, r_lo)
            u_sc[d["hh"], d["ci"]] = uw[:, :LANE]
            wq_sc[d["hh"], d["ci"]] = jnp.concatenate([uw[:, LANE:], d["qe"]], axis=0)
        return carry

    lax.fori_loop(0, nc // cpi, phase1, 0)

    og = og_ref[...]

    def phase2(ci, carry):
        r0 = pl.multiple_of(ci * c, c)
        s = [s_sc[hh] for hh in range(hb)]
        ws_qs = [_mm(wq_sc[hh, ci], s[hh]) for hh in range(hb)]
        v_new = [u_sc[hh, ci] - ws_qs[hh][:c] for hh in range(hb)]
        for hh in range(hb):
            s_sc[hh] = ge_sc[hh, ci][0:1, :] * s[hh] + _mm_tn(kd_sc[hh, ci], v_new[hh])
        for hh in range(hb):
            sl = slice(hh * LANE, (hh + 1) * LANE)
            o = ws_qs[hh][c:] + _mm(qk_sc[hh, ci], v_new[hh])
            o = o * lax.rsqrt(jnp.mean(o * o, axis=-1, keepdims=True) + EPS) * og
            o_ref[pl.ds(r0, c), sl] = (o * _silu(z_ref[pl.ds(r0, c), sl])).astype(BF16)
        return carry

    lax.fori_loop(0, nc, phase2, 0)
    sfin_ref[0] = s_sc[...]


def _delta(proj, bsz, t_len, hist8, s0, conv_w, a_log, dt_bias, onorm_g, *, tb, hb):
    c = CHUNK if t_len % CHUNK == 0 else t_len
    nb = t_len // tb
    hw = hb * LANE
    rows = bsz * t_len

    def col(off):
        return lambda b, g, t: (b * nb + t, off // hw + g)

    def hcol(off):
        return lambda b, g, t: (b, 0, off // hw + g)

    def wcol(off):
        return lambda b, g, t: (0, off // hw + g)

    smem = pl.BlockSpec(memory_space=pltpu.SMEM)
    nc = tb // c
    kern = functools.partial(_delta_kernel, c=c, tb=tb, hb=hb, cpi=2 if nc % 2 == 0 else 1)
    return pl.pallas_call(
        kern,
        grid=(bsz, H_A // hb, nb),
        in_specs=[
            smem, smem,
            pl.BlockSpec((tb, hw), col(COL_QKV)),
            pl.BlockSpec((tb, hw), col(COL_QKV + 1024)),
            pl.BlockSpec((tb, hw), col(COL_QKV + 2048)),
            pl.BlockSpec((tb, hw), col(COL_ZA)),
            pl.BlockSpec((tb, LANE), lambda b, g, t: (b * nb + t, COL_SMALL // LANE)),
            pl.BlockSpec((1, 8, hw), hcol(0)),
            pl.BlockSpec((1, 8, hw), hcol(1024)),
            pl.BlockSpec((1, 8, hw), hcol(2048)),
            pl.BlockSpec((SHORT_CONV, hw), wcol(0)),
            pl.BlockSpec((SHORT_CONV, hw), wcol(1024)),
            pl.BlockSpec((SHORT_CONV, hw), wcol(2048)),
            pl.BlockSpec((1, LANE), lambda b, g, t: (0, 0)),
            pl.BlockSpec((1, hb, DK_A, DK_A), lambda b, g, t: (b, g, 0, 0)),
        ],
        out_specs=[
            pl.BlockSpec((tb, hw), lambda b, g, t: (b * nb + t, g)),
            pl.BlockSpec((1, hb, DK_A, DK_A), lambda b, g, t: (b, g, 0, 0)),
        ],
        out_shape=[
            jax.ShapeDtypeStruct((rows, 1024), BF16),
            jax.ShapeDtypeStruct((bsz, H_A, DK_A, DK_A), F32),
        ],
        scratch_shapes=[
            pltpu.VMEM((hb, DK_A, DK_A), F32),
            pltpu.VMEM((tb + 8, hw), F32),
            pltpu.VMEM((tb + 8, hw), F32),
            pltpu.VMEM((tb + 8, hw), F32),
            pltpu.VMEM((hb, nc, c, LANE), F32),
            pltpu.VMEM((hb, nc, 2 * c, LANE), F32),
            pltpu.VMEM((hb, nc, c, LANE), F32),
            pltpu.VMEM((hb, nc, c, c), F32),
            pltpu.VMEM((hb, nc, 8, LANE), F32),
        ],
        compiler_params=_cparams(("parallel", "parallel", "arbitrary")),
        name="delta",
    )(a_log, dt_bias, proj, proj, proj, proj, proj, hist8, hist8, hist8, conv_w, conv_w, conv_w, onorm_g, s0)


def _convb_kernel(glu_ref, halo_ref, hist_ref, z_ref, w_ref, bias_ref, lng_ref, lnb_ref, o_ref, tail_ref, xp_ref,
                  y_ref, *, tb):
    j = pl.program_id(1)

    @pl.when(j == 0)
    def _():
        xp_ref[0:HALO, :] = hist_ref[0]

    @pl.when(j > 0)
    def _():
        hl = halo_ref[...]
        xp_ref[0:HALO, :] = hl[:, :C_B] * _sigmoid(hl[:, C_B:])

    gl = glu_ref[...]
    xp_ref[HALO:HALO + tb, :] = gl[:, :C_B] * _sigmoid(gl[:, C_B:])
    tail_ref[0] = xp_ref[tb:tb + HALO, :]

    rc_n = min(tb, 64)
    cc_n = LANE
    base = HALO - (CONV_B - 1)
    win_n = rc_n + HALO
    for cc in range(C_B // cc_n):
        cs = slice(cc * cc_n, (cc + 1) * cc_n)
        for rc in range(tb // rc_n):
            win = xp_ref[rc * rc_n:rc * rc_n + win_n, cs]
            acc = jnp.zeros((rc_n, cc_n), F32)
            for ph in range(8):
                wph = win if ph == 0 else pltpu.roll(win, win_n - ph, 0)
                for a in range(win_n // 8):
                    k = 8 * a + ph - base
                    if 0 <= k < CONV_B and 8 * a + rc_n <= win_n:
                        acc = acc + w_ref[k:k + 1, cs] * wph[8 * a:8 * a + rc_n, :]
            y_ref[rc * rc_n:(rc + 1) * rc_n, cs] = acc

    y = y_ref[...] + bias_ref[...]
    mu = jnp.mean(y, axis=-1, keepdims=True)
    yc = y - mu
    var = jnp.mean(yc * yc, axis=-1, keepdims=True)
    yn = yc * lax.rsqrt(var + EPS) * lng_ref[...] + lnb_ref[...]
    o_ref[...] = (_silu(yn) * _silu(z_ref[...])).astype(BF16)


def _convb(proj, bsz, t_len, hist32, w32, bias, ln_g, ln_b, *, tb):
    nb = t_len // tb
    rows = bsz * t_len
    rb = tb // HALO
    vec = pl.BlockSpec((1, C_B), lambda b, j: (0, 0))
    kern = functools.partial(_convb_kernel, tb=tb)
    return pl.pallas_call(
        kern,
        grid=(bsz, nb),
        in_specs=[
            pl.BlockSpec((tb, 2 * C_B), lambda b, j: (b * nb + j, COL_GLU // (2 * C_B))),
            pl.BlockSpec((HALO, 2 * C_B), lambda b, j: (jnp.maximum((b * nb + j) * rb - 1, 0), COL_GLU // (2 * C_B))),
            pl.BlockSpec((1, HALO, C_B), lambda b, j: (b, 0, 0)),
            pl.BlockSpec((tb, C_B), lambda b, j: (b * nb + j, COL_ZB // C_B)),
            pl.BlockSpec((HALO, C_B), lambda b, j: (0, 0)),
            vec, vec, vec,
        ],
        out_specs=[
            pl.BlockSpec((tb, C_B), lambda b, j: (b * nb + j, 0)),
            pl.BlockSpec((1, HALO, C_B), lambda b, j: (b, 0, 0)),
        ],
        out_shape=[
            jax.ShapeDtypeStruct((rows, C_B), BF16),
            jax.ShapeDtypeStruct((bsz, HALO, C_B), F32),
        ],
        scratch_shapes=[pltpu.VMEM((HALO + tb, C_B), F32), pltpu.VMEM((tb, C_B), F32)],
        compiler_params=_cparams(("parallel", "arbitrary")),
        name="convb",
    )(proj, proj, hist32, proj, w32, bias, ln_g, ln_b)


def _tile_lanes(x, n):
    return x if n == 1 else jnp.concatenate([x] * n, axis=1)


def _dsa_kernel(qi_ref, sm_ref, kit_ref, q_ref, k_ref, v_ref, z_ref, o_ref, key_ref, wb_ref, m_ref, l_ref, acc_ref,
                *, tq, kb, n_keys, n_valid, pos0, topk):
    i = pl.program_id(1)
    nrep = kb // LANE
    last_pos = pos0 + (i + 1) * tq - 1
    lim_last = jnp.minimum((last_pos // CHUNK + 1) * CHUNK, n_valid)
    nkb = jnp.minimum((lim_last + kb - 1) // kb, n_keys // kb)
    qpos = pos0 + i * tq + lax.broadcasted_iota(I32, (tq, 1), 0)
    lim_row = jnp.minimum(((qpos >> 6) + 1) << 6, n_valid)
    lane_kb = lax.broadcasted_iota(I32, (1, kb), 1)
    lane_tq = lax.broadcasted_iota(I32, (tq, LANE), 1)

    sm = sm_ref[...]
    for j in range(H_I):
        wb_ref[j] = jnp.broadcast_to(sm[:, SM_WI + j:SM_WI + j + 1] * (H_I ** -0.5), (tq, LANE))

    def p1(kblk, carry):
        kt = kit_ref[0, kblk]
        acc = jnp.zeros((tq, kb), F32)
        for j in range(H_I):
            lg = jnp.dot(qi_ref[:, j * D_I:(j + 1) * D_I], kt, preferred_element_type=F32)
            acc = acc + _tile_lanes(wb_ref[j], nrep) * jnp.maximum(lg, 0.0)
        adm = (kblk * kb + lane_kb) < lim_row
        bits = lax.bitcast_convert_type(jnp.where(adm, acc, NEG_INF), I32)
        key_ref[kblk] = bits ^ ((bits >> 31) & 0x7FFFFFFF)
        return carry

    lax.fori_loop(0, nkb, p1, 0)

    def count_ge(cand):
        cb = jnp.broadcast_to(cand, (tq, LANE))

        def body(kblk, cnt):
            blk = key_ref[kblk]
            for t in range(nrep):
                cnt = cnt + jnp.where(blk[:, t * LANE:(t + 1) * LANE] >= cb, 1.0, 0.0)
            return cnt

        cnt = lax.fori_loop(0, nkb, body, jnp.zeros((tq, LANE), F32))
        return jnp.sum(cnt, axis=1, keepdims=True)

    def bisect(p, t_u):
        cand_u = t_u | jnp.left_shift(jnp.int32(1), 31 - p)
        cnt = count_ge(cand_u ^ INT_MIN)
        return jnp.where(cnt >= topk, cand_u, t_u)

    t_u = lax.fori_loop(0, 32, bisect, jnp.zeros((tq, 1), I32))
    thr = jnp.maximum(t_u ^ INT_MIN, NEG_INF_KEY + 1)
    cnt_ge = count_ge(thr)

    @pl.when(jnp.max(cnt_ge) > topk)
    def _():
        need = topk - count_ge(thr + 1)
        thr_b = jnp.broadcast_to(thr, (tq, LANE))

        def count_tie_lt(cidx):
            cb = jnp.broadcast_to(cidx, (tq, LANE))

            def body(kblk, cnt):
                blk = key_ref[kblk]
                for t in range(nrep):
                    idx = kblk * kb + t * LANE + lane_tq
                    hit = (blk[:, t * LANE:(t + 1) * LANE] == thr_b) & (idx < cb)
                    cnt = cnt + jnp.where(hit, 1.0, 0.0)
                return cnt

            cnt = lax.fori_loop(0, nkb, body, jnp.zeros((tq, LANE), F32))
            return jnp.sum(cnt, axis=1, keepdims=True)

        nbits = max(n_keys - 1, 1).bit_length()

        def bis_idx(p, cm):
            cand = cm | jnp.left_shift(jnp.int32(1), nbits - 1 - p)
            return jnp.where(count_tie_lt(cand) < need, cand, cm)

        cm = lax.fori_loop(0, nbits, bis_idx, jnp.zeros((tq, 1), I32))
        cm_b = jnp.broadcast_to(cm, (tq, LANE))

        def rewrite(kblk, carry):
            blk = key_ref[kblk]
            parts = []
            for t in range(nrep):
                idx = kblk * kb + t * LANE + lane_tq
                part = blk[:, t * LANE:(t + 1) * LANE]
                parts.append(jnp.where((part == thr_b) & (idx > cm_b), thr_b - 1, part))
            key_ref[kblk] = _tile_lanes(parts[0], 1) if nrep == 1 else jnp.concatenate(parts, axis=1)
            return carry

        lax.fori_loop(0, nkb, rewrite, 0)

    thr_kb = _tile_lanes(jnp.broadcast_to(thr, (tq, LANE)), nrep)
    m_ref[...] = jnp.full(m_ref.shape, NEG_INF, F32)
    l_ref[...] = jnp.zeros(l_ref.shape, F32)
    acc_ref[...] = jnp.zeros(acc_ref.shape, F32)
    group = H_C // H_KV

    def p3(kblk, carry):
        k0 = pl.multiple_of(kblk * kb, kb)
        sel = key_ref[kblk] >= thr_kb
        for g in range(H_KV):
            gs = slice(g * DH_C, (g + 1) * DH_C)
            kg = k_ref[0, pl.ds(k0, kb), gs]
            vg = v_ref[0, pl.ds(k0, kb), gs]
            for hq in range(group):
                h = g * group + hq
                s = lax.dot_general(q_ref[:, h * DH_C:(h + 1) * DH_C], kg, (((1,), (1,)), ((), ())),
                                    preferred_element_type=F32) * (DH_C ** -0.5)
                s = jnp.where(sel, s, NEG_INF)
                m_prev = m_ref[h]
                m_new = jnp.maximum(m_prev, jnp.max(s, axis=1, keepdims=True))
                m_safe = jnp.where(m_new == NEG_INF, 0.0, m_new)
                alpha = jnp.exp(m_prev - m_safe)
                p = jnp.exp(s - _tile_lanes(m_safe, nrep))
                l_ref[h] = alpha * l_ref[h] + jnp.sum(p, axis=1, keepdims=True)
                acc_ref[h] = alpha * acc_ref[h] + jnp.dot(p.astype(BF16), vg, preferred_element_type=F32)
                m_ref[h] = m_new
        return carry

    lax.fori_loop(0, nkb, p3, 0)

    for h in range(H_C):
        hs = slice(h * DH_C, (h + 1) * DH_C)
        o = acc_ref[h] / l_ref[h]
        o_ref[:, hs] = (o * _silu(z_ref[:, hs])).astype(BF16)


def _dsa(proj, qi_r, q_r, kit, k_all, v_all, bsz, t_len, n_valid, pos0, *, tq, kb):
    n_keys = k_all.shape[1]
    nq = t_len // tq
    rows = bsz * t_len
    topk = min(TOPK, n_valid // 4)
    kern = functools.partial(_dsa_kernel, tq=tq, kb=kb, n_keys=n_keys, n_valid=n_valid, pos0=pos0, topk=topk)
    return pl.pallas_call(
        kern,
        grid=(bsz, nq),
        in_specs=[
            pl.BlockSpec((tq, 512), lambda b, i: (b * nq + i, 0)),
            pl.BlockSpec((tq, LANE), lambda b, i: (b * nq + i, COL_SMALL // LANE)),
            pl.BlockSpec((1, n_keys // kb, D_I, kb), lambda b, i: (b, 0, 0, 0)),
            pl.BlockSpec((tq, 1024), lambda b, i: (b * nq + i, 0)),
            pl.BlockSpec((1, n_keys, 256), lambda b, i: (b, 0, 0)),
            pl.BlockSpec((1, n_keys, 256), lambda b, i: (b, 0, 0)),
            pl.BlockSpec((tq, 1024), lambda b, i: (b * nq + i, COL_ZC // 1024)),
        ],
        out_specs=pl.BlockSpec((tq, 1024), lambda b, i: (b * nq + i, 0)),
        out_shape=jax.ShapeDtypeStruct((rows, 1024), BF16),
        scratch_shapes=[
            pltpu.VMEM((n_keys // kb, tq, kb), I32),
            pltpu.VMEM((H_I, tq, LANE), F32),
            pltpu.VMEM((H_C, tq, LANE), F32),
            pltpu.VMEM((H_C, tq, LANE), F32),
            pltpu.VMEM((H_C, tq, DH_C), F32),
        ],
        compiler_params=_cparams(("parallel", "arbitrary")),
        name="dsa",
    )(qi_r, proj, kit, q_r, k_all, v_all, proj)


SUB = 128
VT_ONES = 16


def _tree(parts, op):
    parts = list(parts)
    while len(parts) > 1:
        nxt = [op(parts[n], parts[n + 1]) for n in range(0, len(parts) - 1, 2)]
        if len(parts) % 2:
            nxt.append(parts[-1])
        parts = nxt
    return parts[0]


def _fold8(x, op):
    return _tree([x[r:r + 8, :] for r in range(0, x.shape[0], 8)], op)


def _dsa_t_kernel(qi_ref, wt_ref, ki_ref, q_ref, k_ref, vt_ref, z_ref, o_ref, key_ref, khi_ref, klo_ref, bias_ref,
                  s_ref, p_ref, m_ref, acc_ref, *, tq, kb, n_keys, n_valid, pos0, topk):
    i = pl.program_id(1)
    nsub = kb // SUB
    last_pos = pos0 + (i + 1) * tq - 1
    lim_last = jnp.minimum((last_pos // CHUNK + 1) * CHUNK, n_valid)
    nkb = jnp.minimum((lim_last + kb - 1) // kb, n_keys // kb)
    qpos = pos0 + i * tq + lax.broadcasted_iota(I32, (1, tq), 1)
    lim = jnp.minimum(((qpos >> 6) + 1) << 6, n_valid)
    sub_iota = lax.broadcasted_iota(I32, (SUB, 1), 0)
    wt = wt_ref[0] * (H_I ** -0.5)
    nt = (((1,), (1,)), ((), ()))

    def p1(kblk, carry):
        for sb in range(nsub):
            r0 = pl.multiple_of(kblk * kb + sb * SUB, SUB)
            kis = ki_ref[0, pl.ds(r0, SUB), :]
            acc = jnp.zeros((SUB, tq), F32)
            for j in range(H_I):
                lg = lax.dot_general(kis, qi_ref[:, j * D_I:(j + 1) * D_I], nt, preferred_element_type=F32)
                acc = acc + wt[j:j + 1, :] * jnp.maximum(lg, 0.0)
            bits = lax.bitcast_convert_type(jnp.where((r0 + sub_iota) < lim, acc, NEG_INF), I32)
            key = bits ^ ((bits >> 31) & 0x7FFFFFFF)
            ss = slice(sb * SUB, (sb + 1) * SUB)
            key_ref[kblk, ss, :] = key
            khi_ref[kblk, ss, :] = (key >> 16).astype(I16)
            klo_ref[kblk, ss, :] = ((key & 0xFFFF) - 32768).astype(I16)
        return carry

    lax.fori_loop(0, nkb, p1, 0)

    def count(pred):
        def body(kblk, cnt):
            parts = [cnt]
            for sb in range(nsub):
                blk = key_ref[kblk, sb * SUB:(sb + 1) * SUB, :]
                parts.append(_fold8(jnp.where(pred(blk, kblk * kb + sb * SUB), 1.0, 0.0), jnp.add))
            return _tree(parts, jnp.add)

        cnt = lax.fori_loop(0, nkb, body, jnp.zeros((8, tq), F32))
        return jnp.sum(cnt, axis=0, keepdims=True)

    def count_ge(cand):
        return count(lambda blk, _: blk >= cand)

    one16 = jnp.ones((16, tq), I16)
    zero16 = jnp.zeros((16, tq), I16)

    def count16_ge(h_ref, cand_s):
        cb = jnp.broadcast_to(cand_s, (16, tq)).astype(I16)

        def body(kblk, cnt):
            parts = [cnt]
            for sb in range(nsub):
                blk = h_ref[kblk, sb * SUB:(sb + 1) * SUB, :]
                parts.append(_tree([jnp.where(blk[r:r + 16, :] >= cb, one16, zero16) for r in range(0, SUB, 16)],
                                   jnp.add))
            return _tree(parts, jnp.add)

        cnt = lax.fori_loop(0, nkb, body, zero16)
        return jnp.sum(cnt.astype(I32).astype(F32), axis=0, keepdims=True)

    def bisect16(h_ref):
        def step(p, t_u):
            cand_u = t_u | jnp.left_shift(jnp.int32(1), 15 - p)
            return jnp.where(count16_ge(h_ref, cand_u - 32768) >= topk, cand_u, t_u)

        return lax.fori_loop(0, 16, step, jnp.zeros((1, tq), I32))

    th_s = bisect16(khi_ref) - 32768
    th16 = jnp.broadcast_to(th_s, (16, tq)).astype(I16)

    def mask_low(kblk, carry):
        for sb in range(nsub):
            for r in range(0, SUB, 16):
                rs = slice(sb * SUB + r, sb * SUB + r + 16)
                hi = khi_ref[kblk, rs, :]
                other = jnp.where(hi > th16, jnp.int16(32767), jnp.int16(-32768))
                klo_ref[kblk, rs, :] = jnp.where(hi == th16, klo_ref[kblk, rs, :], other)
        return carry

    lax.fori_loop(0, nkb, mask_low, 0)
    thr = th_s * 65536 + bisect16(klo_ref)
    thr = jnp.maximum(thr, NEG_INF_KEY + 1)
    cnt_ge = count_ge(thr)

    @pl.when(jnp.max(cnt_ge) > topk)
    def _():
        need = topk - count_ge(thr + 1)
        nbits = max(n_keys - 1, 1).bit_length()

        def bis_idx(p, cm):
            cand = cm | jnp.left_shift(jnp.int32(1), nbits - 1 - p)
            below = count(lambda blk, k0: (blk == thr) & ((k0 + sub_iota) < cand))
            return jnp.where(below < need, cand, cm)

        cm = lax.fori_loop(0, nbits, bis_idx, jnp.zeros((1, tq), I32))

        def rewrite(kblk, carry):
            for sb in range(nsub):
                blk = key_ref[kblk, sb * SUB:(sb + 1) * SUB, :]
                drop = (blk == thr) & ((kblk * kb + sb * SUB + sub_iota) > cm)
                key_ref[kblk, sb * SUB:(sb + 1) * SUB, :] = jnp.where(drop, thr - 1, blk)
            return carry

        lax.fori_loop(0, nkb, rewrite, 0)

    m_ref[...] = jnp.full(m_ref.shape, NEG_INF, F32)
    acc_ref[...] = jnp.zeros(acc_ref.shape, F32)
    group = H_C // H_KV
    c_exp = (DH_C ** -0.5) * math.log2(math.e)
    va = DH_C + VT_ONES

    def p3(kblk, carry):
        k0 = pl.multiple_of(kblk * kb, kb)
        for sb in range(nsub):
            ss = slice(sb * SUB, (sb + 1) * SUB)
            bias_ref[ss, :] = jnp.where(key_ref[kblk, ss, :] >= thr, 0.0, NEG_INF)

        def logits(h):
            g = h // group
            mx = []
            for sb in range(nsub):
                ss = slice(sb * SUB, (sb + 1) * SUB)
                ks = k_ref[0, pl.ds(k0 + sb * SUB, SUB), g * DH_C:(g + 1) * DH_C]
                s = lax.dot_general(ks, q_ref[:, h * DH_C:(h + 1) * DH_C], nt, preferred_element_type=F32)
                s = s + bias_ref[ss, :]
                s_ref[h % 2, ss, :] = s
                mx.append(_fold8(s, jnp.maximum))
            return jnp.max(_tree(mx, jnp.maximum), axis=0, keepdims=True)

        m_cur = logits(0)
        for h in range(H_C):
            m_nxt = logits(h + 1) if h + 1 < H_C else None
            g = h // group
            m_prev = m_ref[h][0:1, :]
            m_new = jnp.maximum(m_prev, m_cur)
            m_safe = jnp.where(m_new == NEG_INF, 0.0, m_new)
            alpha = jnp.exp2((m_prev - m_safe) * c_exp)
            for sb in range(nsub):
                ss = slice(sb * SUB, (sb + 1) * SUB)
                p_ref[ss, :] = jnp.exp2((s_ref[h % 2, ss, :] - m_safe) * c_exp).astype(BF16)
            pv = jnp.dot(vt_ref[0, kblk, g * va:(g + 1) * va, :], p_ref[...], preferred_element_type=F32)
            acc_ref[h] = alpha * acc_ref[h] + pv
            m_ref[h] = jnp.broadcast_to(m_new, (8, tq))
            m_cur = m_nxt
        return carry

    lax.fori_loop(0, nkb, p3, 0)

    for h in range(H_C):
        hs = slice(h * DH_C, (h + 1) * DH_C)
        o = (acc_ref[h, 0:DH_C, :] / acc_ref[h, DH_C:DH_C + 1, :]).T
        o_ref[:, hs] = (o * _silu(z_ref[:, hs])).astype(BF16)


def _dsa_t(proj, qi_r, q_r, wi_t, ki_all, k_all, vt, bsz, t_len, n_valid, pos0, *, tq, kb):
    n_keys = k_all.shape[1]
    nq = t_len // tq
    rows = bsz * t_len
    topk = min(TOPK, n_valid // 4)
    kern = functools.partial(_dsa_t_kernel, tq=tq, kb=kb, n_keys=n_keys, n_valid=n_valid, pos0=pos0, topk=topk)
    return pl.pallas_call(
        kern,
        grid=(bsz, nq),
        in_specs=[
            pl.BlockSpec((tq, 512), lambda b, i: (b * nq + i, 0)),
            pl.BlockSpec((1, H_I, tq), lambda b, i: (b * nq + i, 0, 0)),
            pl.BlockSpec((1, n_keys, D_I), lambda b, i: (b, 0, 0)),
            pl.BlockSpec((tq, 1024), lambda b, i: (b * nq + i, 0)),
            pl.BlockSpec((1, n_keys, 256), lambda b, i: (b, 0, 0)),
            pl.BlockSpec((1, n_keys // kb, H_KV * (DH_C + VT_ONES), kb), lambda b, i: (b, 0, 0, 0)),
            pl.BlockSpec((tq, 1024), lambda b, i: (b * nq + i, COL_ZC // 1024)),
        ],
        out_specs=pl.BlockSpec((tq, 1024), lambda b, i: (b * nq + i, 0)),
        out_shape=jax.ShapeDtypeStruct((rows, 1024), BF16),
        scratch_shapes=[
            pltpu.VMEM((n_keys // kb, kb, tq), I32),
            pltpu.VMEM((n_keys // kb, kb, tq), I16),
            pltpu.VMEM((n_keys // kb, kb, tq), I16),
            pltpu.VMEM((kb, tq), F32),
            pltpu.VMEM((2, kb, tq), F32),
            pltpu.VMEM((kb, tq), BF16),
            pltpu.VMEM((H_C, 8, tq), F32),
            pltpu.VMEM((H_C, DH_C + VT_ONES, tq), F32),
        ],
        compiler_params=_cparams(("parallel", "arbitrary")),
        name="dsa_t",
    )(qi_r, wi_t, ki_all, q_r, k_all, vt, proj)


def _merge_kernel(oa_ref, ub_ref, oc_ref, gate_ref, x_ref, woa_ref, wpw_ref, woc_ref, wout_ref, fg_ref, xo_ref,
                  *rest, final):
    ya = jnp.dot(oa_ref[...], woa_ref[0], preferred_element_type=F32)
    mix = _sigmoid(gate_ref[:, 0:D_MODEL]) * ya
    yb = jnp.dot(ub_ref[...], wpw_ref[0], preferred_element_type=F32)
    mix = mix + _sigmoid(gate_ref[:, D_MODEL:2 * D_MODEL]) * yb
    yc = jnp.dot(oc_ref[...], woc_ref[0], preferred_element_type=F32)
    mix = mix + _sigmoid(gate_ref[:, 2 * D_MODEL:3 * D_MODEL]) * yc
    xn = x_ref[...] + jnp.dot(mix.astype(BF16), wout_ref[0], preferred_element_type=F32)
    xo_ref[...] = xn
    if final:
        ms = jnp.mean(xn * xn, axis=-1, keepdims=True)
        rest[0][...] = xn * lax.rsqrt(ms + EPS) * fg_ref[...]


def _merge(o_a, u_b, o_c, proj, x2d, w_oa, w_pw, w_oc, w_out, final_g, layer, final):
    rows = x2d.shape[0]
    tm = _pick(rows, (512, 256, 128, 64, 32, 16, 8))
    act = pl.BlockSpec((tm, D_MODEL), lambda i: (i, 0))
    wsp = pl.BlockSpec((1, D_MODEL, D_MODEL), lambda i: (layer, 0, 0))
    out_specs = [act]
    out_shape = [jax.ShapeDtypeStruct((rows, D_MODEL), F32)]
    if final:
        out_specs.append(act)
        out_shape.append(jax.ShapeDtypeStruct((rows, D_MODEL), F32))
    res = pl.pallas_call(
        functools.partial(_merge_kernel, final=final),
        grid=(rows // tm,),
        in_specs=[act, act, act, pl.BlockSpec((tm, 3 * D_MODEL), lambda i: (i, COL_GATE // (3 * D_MODEL))), act,
                  wsp, wsp, wsp, wsp, pl.BlockSpec((1, D_MODEL), lambda i: (0, 0))],
        out_specs=out_specs,
        out_shape=out_shape,
        compiler_params=_cparams(("parallel",)),
        name="merge",
    )(o_a, u_b, o_c, proj, x2d, w_oa, w_pw, w_oc, w_out, final_g)
    return res if final else (res[0], None)


def _permute_w_in(w_in):
    offs = [0]
    for s in IN_SIZES:
        offs.append(offs[-1] + s)
    (qkv, a_in, b_in, z_a, glu, z_b, q_c, k_c, v_c, qi, ki, wi, z_c, gate) = [
        w_in[..., offs[n]:offs[n + 1]] for n in range(len(IN_SIZES))]
    pad = jnp.zeros(w_in.shape[:-1] + (SM_KI - SM_WI - H_I,), w_in.dtype)
    small = jnp.concatenate([a_in, b_in, wi, pad, ki], axis=-1)
    w = jnp.concatenate([qkv, gate, glu, z_a, z_b, q_c, z_c, qi, k_c, v_c, small], axis=-1)
    assert w.shape[-1] == NP
    return w.astype(BF16)


def _stream_layer(x2d, layer, final, cfg, state, prm):
    bsz, t_len, pos0 = cfg["bsz"], cfg["t_len"], cfg["pos0"]
    proj = _inproj(x2d, prm["norm_g"], prm["w_in"], layer)

    o_a, delta_new = _delta(proj, bsz, t_len, state["conv_a8"], state["delta"], prm["conv_a_w"][layer],
                            prm["a_log"][layer], prm["dt_bias"][layer], prm["onorm_a_g"][layer][None, :],
                            tb=cfg["delta_tb"], hb=cfg["delta_hb"])
    conv_a_new = proj.reshape(bsz, t_len, NP)[:, t_len - (SHORT_CONV - 1):, COL_QKV:COL_QKV + 3 * D_MODEL]

    u_b, tail_b = _convb(proj, bsz, t_len, state["conv_b32"], prm["conv_b_w32"][layer],
                         prm["conv_b_bias"][layer][None, :], prm["ln_b_g"][layer][None, :],
                         prm["ln_b_b"][layer][None, :], tb=cfg["convb_tb"])
    conv_b_new = tail_b[:, HALO - (CONV_B - 1):, :]

    q_r, k_r, qi_r, ki_r = _rope(proj, bsz, t_len, pos0)
    v_new = proj[:, COL_VC:COL_VC + 256].reshape(bsz, t_len, 256)
    n_valid = pos0 + t_len
    kb = cfg["dsa_kb"]
    n_keys = -(-n_valid // kb) * kb
    parts_k, parts_v, parts_ki = [k_r], [v_new], [ki_r]
    if pos0:
        parts_k.insert(0, state["past_k"])
        parts_v.insert(0, state["past_v"])
        parts_ki.insert(0, state["past_ki"])
    if n_keys > n_valid:
        parts_k.append(jnp.zeros((bsz, n_keys - n_valid, 256), F32))
        parts_v.append(jnp.zeros((bsz, n_keys - n_valid, 256), F32))
        parts_ki.append(jnp.zeros((bsz, n_keys - n_valid, D_I), F32))
    k_all = jnp.concatenate(parts_k, axis=1).astype(BF16) if len(parts_k) > 1 else k_r.astype(BF16)
    v_all = jnp.concatenate(parts_v, axis=1).astype(BF16) if len(parts_v) > 1 else v_new.astype(BF16)
    ki_all = (jnp.concatenate(parts_ki, axis=1) if len(parts_ki) > 1 else ki_r).astype(BF16)
    tq = cfg["dsa_tq"]
    if tq % LANE == 0:
        wi_t = proj[:, COL_SMALL + SM_WI:COL_SMALL + SM_WI + H_I].reshape(bsz * t_len // tq, tq, H_I).transpose(0, 2, 1)
        vt = v_all.reshape(bsz, n_keys // kb, kb, H_KV, DH_C).transpose(0, 1, 3, 4, 2)
        vt = jnp.concatenate([vt, jnp.ones((bsz, n_keys // kb, H_KV, VT_ONES, kb), BF16)], axis=3)
        vt = vt.reshape(bsz, n_keys // kb, H_KV * (DH_C + VT_ONES), kb)
        o_c = _dsa_t(proj, qi_r, q_r, wi_t, ki_all, k_all, vt, bsz, t_len, n_valid, pos0, tq=tq, kb=kb)
    else:
        kit = ki_all.reshape(bsz, n_keys // kb, kb, D_I).transpose(0, 1, 3, 2)
        o_c = _dsa(proj, qi_r, q_r, kit, k_all, v_all, bsz, t_len, n_valid, pos0, tq=tq, kb=kb)

    x_new, y_fin = _merge(o_a, u_b, o_c, proj, x2d, prm["w_o_a"], prm["w_pw2_b"], prm["w_o_c"], prm["w_out"],
                          prm["final_norm_g"], layer, final)
    new_state = (k_r.reshape(bsz, t_len, H_KV, DH_C), v_new.reshape(bsz, t_len, H_KV, DH_C), ki_r, conv_a_new,
                 delta_new, conv_b_new)
    return x_new, y_fin, new_state


def _run(x_prompt, x_sample, cache_k, cache_v, cache_idx_k, state_conv_a, state_delta, state_conv_b, prm, cfg_p,
         cfg_s):
    n_layers = prm["norm_g"].shape[0]
    bp, tp = cfg_p["bsz"], cfg_p["t_len"]
    bs, ts = cfg_s["bsz"], cfg_s["t_len"]
    past = cfg_s["pos0"]
    xp = x_prompt.reshape(bp * tp, D_MODEL)
    xs = x_sample.reshape(bs * ts, D_MODEL)
    zero_state = {
        "conv_a8": jnp.zeros((bp, 8, 3 * D_MODEL), F32),
        "delta": jnp.zeros((bp, H_A, DK_A, DK_A), F32),
        "conv_b32": jnp.zeros((bp, HALO, C_B), F32),
    }
    st_p, st_s = [], []
    yp = ys = None
    for layer in range(n_layers):
        final = layer == n_layers - 1
        xp, yp, sp = _stream_layer(xp, layer, final, cfg_p, zero_state, prm)
        s_state = {
            "conv_a8": jnp.pad(state_conv_a[layer], ((0, 0), (8 - (SHORT_CONV - 1), 0), (0, 0))),
            "delta": state_delta[layer],
            "conv_b32": jnp.pad(state_conv_b[layer], ((0, 0), (HALO - (CONV_B - 1), 0), (0, 0))),
            "past_k": cache_k[layer].reshape(bs, past, 256),
            "past_v": cache_v[layer].reshape(bs, past, 256),
            "past_ki": cache_idx_k[layer],
        }
        xs, ys, ss = _stream_layer(xs, layer, final, cfg_s, s_state, prm)
        st_p.append(sp)
        st_s.append(ss)

    def stack(states, n):
        return jnp.stack([s[n] for s in states], axis=0)

    return ((yp.reshape(bp, tp, D_MODEL), ys.reshape(bs, ts, D_MODEL))
            + tuple(stack(st_p, n) for n in range(6)) + tuple(stack(st_s, n) for n in range(6)))


def _prep_params(norm_g, w_in, conv_a_w, a_log, dt_bias, onorm_a_g, w_o_a, conv_b_w, conv_b_bias, ln_b_g, ln_b_b,
                 w_pw2_b, w_o_c, w_out, final_norm_g):
    return {
        "norm_g": norm_g[:, None, :], "w_in": _permute_w_in(w_in), "conv_a_w": conv_a_w, "a_log": a_log, "dt_bias": dt_bias,
        "onorm_a_g": onorm_a_g, "w_o_a": w_o_a.astype(BF16),
        "conv_b_w32": jnp.pad(conv_b_w, ((0, 0), (0, HALO - CONV_B), (0, 0))),
        "conv_b_bias": conv_b_bias, "ln_b_g": ln_b_g, "ln_b_b": ln_b_b, "w_pw2_b": w_pw2_b.astype(BF16),
        "w_o_c": w_o_c.astype(BF16), "w_out": w_out.astype(BF16), "final_norm_g": final_norm_g[None, :],
    }


def _stream_cfg(bsz, t_len, pos0):
    c = CHUNK if t_len % CHUNK == 0 else t_len
    n_valid = pos0 + t_len
    if t_len % 256 == 0:
        tq, kb = 256, 512
    else:
        tq = t_len
        kb = 384 if n_valid > 384 else LANE
    delta_tb = _pick(t_len, (512, 256, 128, 64, t_len))
    if delta_tb % c:
        delta_tb = t_len
    return {
        "bsz": bsz, "t_len": t_len, "pos0": pos0,
        "delta_tb": delta_tb, "delta_hb": 4 if t_len >= 512 else H_A,
        "convb_tb": _pick(t_len, (256, 128, 64, 32)),
        "dsa_tq": tq, "dsa_kb": kb,
    }


def kernel(x_prompt, x_sample, cache_k, cache_v, cache_idx_k, state_conv_a, state_delta, state_conv_b, norm_g, w_in,
           conv_a_w, a_log, dt_bias, onorm_a_g, w_o_a, conv_b_w, conv_b_bias, ln_b_g, ln_b_b, w_pw2_b, w_o_c, w_out,
           final_norm_g):
    prm = _prep_params(norm_g, w_in, conv_a_w, a_log, dt_bias, onorm_a_g, w_o_a, conv_b_w, conv_b_bias, ln_b_g,
                       ln_b_b, w_pw2_b, w_o_c, w_out, final_norm_g)
    cfg_p = _stream_cfg(x_prompt.shape[0], x_prompt.shape[1], 0)
    cfg_s = _stream_cfg(x_sample.shape[0], x_sample.shape[1], cache_k.shape[2])
    return _run(x_prompt, x_sample, cache_k, cache_v, cache_idx_k, state_conv_a, state_delta, state_conv_b, prm,
                cfg_p, cfg_s)
```

```python
import functools
import math

import jax
import jax.numpy as jnp
from jax import lax
from jax.experimental import pallas as pl
from jax.experimental.pallas import tpu as pltpu

F32 = jnp.float32
BF16 = jnp.bfloat16
I32 = jnp.int32
I16 = jnp.int16

D_MODEL = 1024
CHUNK = 64
H_A = 8
DK_A = 128
SHORT_CONV = 4
C_B = 1024
CONV_B = 31
H_C = 8
DH_C = 128
H_KV = 2
H_I = 8
D_I = 64
TOPK = 256
ROPE_THETA = 500000.0
ROPE_FRACTION = 4
EPS = 1e-6
IN_SIZES = (3 * D_MODEL, H_A, H_A, D_MODEL, 2 * C_B, C_B, H_C * DH_C, H_KV * DH_C, H_KV * DH_C, H_I * D_I, D_I, H_I,
            H_C * DH_C, 3 * D_MODEL)

LANE = 128
HALO = 32

COL_QKV = 0
COL_GATE = 3072
COL_GLU = 6144
COL_ZA = 8192
COL_ZB = 9216
COL_QC = 10240
COL_ZC = 11264
COL_QI = 12288
COL_KC = 12800
COL_VC = 13056
COL_SMALL = 13312
NP = 13824
SM_A, SM_B, SM_WI, SM_KI = 0, 8, 16, 64

NEG_INF = float("-inf")
NEG_INF_KEY = -2139095041

VMEM_LIMIT = 56 * 1024 * 1024


def _cparams(sem):
    return pltpu.CompilerParams(dimension_semantics=sem, vmem_limit_bytes=VMEM_LIMIT)


def _pick(n, cands):
    for c in cands:
        if n % c == 0:
            return c
    raise ValueError(f"no block size in {cands} divides {n}")


def _sigmoid(x):
    return jax.nn.sigmoid(x)


def _silu(x):
    return x * jax.nn.sigmoid(x)


def _inproj_kernel(x_ref, g_ref, w_ref, o_ref, xn_ref):
    @pl.when(pl.program_id(1) == 0)
    def _():
        x = x_ref[...]
        ms = jnp.mean(x * x, axis=-1, keepdims=True)
        xn_ref[...] = (x * lax.rsqrt(ms + EPS) * g_ref[0]).astype(BF16)

    o_ref[...] = jnp.dot(xn_ref[...], w_ref[0], preferred_element_type=F32)


def _inproj(x2d, g_all, w_all, layer):
    rows = x2d.shape[0]
    tm = _pick(rows, (2048, 1024, 512, 256, 128, 64, 32, 16, 8))
    tn = 768
    return pl.pallas_call(
        _inproj_kernel,
        grid=(rows // tm, NP // tn),
        in_specs=[
            pl.BlockSpec((tm, D_MODEL), lambda i, j: (i, 0)),
            pl.BlockSpec((1, 1, D_MODEL), lambda i, j: (layer, 0, 0)),
            pl.BlockSpec((1, D_MODEL, tn), lambda i, j: (layer, 0, j)),
        ],
        out_specs=pl.BlockSpec((tm, tn), lambda i, j: (i, j)),
        out_shape=jax.ShapeDtypeStruct((rows, NP), F32),
        scratch_shapes=[pltpu.VMEM((tm, D_MODEL), BF16)],
        compiler_params=_cparams(("parallel", "arbitrary")),
        name="inproj",
    )(x2d, g_all, w_all)


def _rope_kernel(q_ref, k_ref, v_ref, qi_ref, sm_ref, cq_ref, sq_ref, ci_ref, si_ref, kst_in, vst_in, kist_in,
                 qo_ref, qio_ref, kst_ref, vst_ref, kist_ref, kbf_ref, kibf_ref, vx_ref, *, transpose_v):
    del kst_in, vst_in, kist_in
    tb = q_ref.shape[0]
    lane = lax.broadcasted_iota(I32, (tb, LANE), 1)
    l64 = lane & 63
    cq, sq, ci, si = cq_ref[...], sq_ref[...], ci_ref[...], si_ref[...]

    def rot_qk(x):
        sw = jnp.where(lane < 16, pltpu.roll(x, LANE - 16, 1), pltpu.roll(x, 16, 1))
        return jnp.where(lane < 32, x * cq + sw * sq, x)

    def rot_idx(x):
        sw = jnp.where(l64 < 8, pltpu.roll(x, LANE - 8, 1), pltpu.roll(x, 8, 1))
        return jnp.where(l64 < 16, x * ci + sw * si, x)

    for h in range(H_C):
        sl = slice(h * LANE, (h + 1) * LANE)
        qo_ref[:, sl] = rot_qk(q_ref[:, sl]).astype(BF16)
    va = DH_C + VT_ONES
    for h in range(H_KV):
        sl = slice(h * LANE, (h + 1) * LANE)
        kr = rot_qk(k_ref[:, sl])
        kst_ref[0, 0, :, sl] = kr
        kbf_ref[0, :, sl] = kr.astype(BF16)
        v = v_ref[:, sl]
        vst_ref[0, 0, :, sl] = v
        if transpose_v:
            vx_ref[0, 0, h * va:h * va + DH_C, :] = v.T.astype(BF16)
            vx_ref[0, 0, h * va + DH_C:(h + 1) * va, :] = jnp.ones((VT_ONES, tb), BF16)
        else:
            vx_ref[0, :, sl] = v.astype(BF16)
    for h in range(H_I * D_I // LANE):
        sl = slice(h * LANE, (h + 1) * LANE)
        qio_ref[:, sl] = (rot_idx(qi_ref[:, sl]) * (D_I ** -0.5)).astype(BF16)
    ki = rot_idx(sm_ref[...])[:, SM_KI:SM_KI + D_I]
    kist_ref[0, 0] = ki
    kibf_ref[0] = ki.astype(BF16)


def _rope_tables(t_len, pos0):
    pos = (pos0 + jnp.arange(t_len)).astype(F32)

    def table(d):
        rot = d // ROPE_FRACTION
        half = rot // 2
        inv = ROPE_THETA ** (-jnp.arange(half, dtype=F32) * 2.0 / rot)
        ang = pos[:, None] * inv[None, :]
        cos, sin = jnp.cos(ang), jnp.sin(ang)
        c = jnp.concatenate([cos, cos, jnp.ones((t_len, d - rot), F32)], axis=1)
        s = jnp.concatenate([-sin, sin, jnp.zeros((t_len, d - rot), F32)], axis=1)
        return jnp.tile(c, (1, LANE // d)), jnp.tile(s, (1, LANE // d))

    cq, sq = table(DH_C)
    ci, si = table(D_I)
    return cq, sq, ci, si


def _rope(proj, stacks, layer, bsz, t_len, pos0, *, tb, transpose_v):
    nb = t_len // tb
    cq, sq, ci, si = _rope_tables(t_len, pos0)
    rows = bsz * t_len
    kst, vst, kist = stacks
    tab = pl.BlockSpec((tb, LANE), lambda b, j: (j, 0))
    anyspec = pl.BlockSpec(memory_space=pl.ANY)
    va2 = H_KV * (DH_C + VT_ONES)
    if transpose_v:
        vx_spec = pl.BlockSpec((1, 1, va2, tb), lambda b, j: (b, j, 0, 0))
        vx_shape = jax.ShapeDtypeStruct((bsz, nb, va2, tb), BF16)
    else:
        vx_spec = pl.BlockSpec((1, tb, 256), lambda b, j: (b, j, 0))
        vx_shape = jax.ShapeDtypeStruct((bsz, t_len, 256), BF16)
    return pl.pallas_call(
        functools.partial(_rope_kernel, transpose_v=transpose_v),
        grid=(bsz, nb),
        in_specs=[
            pl.BlockSpec((tb, 1024), lambda b, j: (b * nb + j, COL_QC // 1024)),
            pl.BlockSpec((tb, 256), lambda b, j: (b * nb + j, COL_KC // 256)),
            pl.BlockSpec((tb, 256), lambda b, j: (b * nb + j, COL_VC // 256)),
            pl.BlockSpec((tb, 512), lambda b, j: (b * nb + j, COL_QI // 512)),
            pl.BlockSpec((tb, LANE), lambda b, j: (b * nb + j, COL_SMALL // LANE)),
            tab, tab, tab, tab,
            anyspec, anyspec, anyspec,
        ],
        out_specs=[
            pl.BlockSpec((tb, 1024), lambda b, j: (b * nb + j, 0)),
            pl.BlockSpec((tb, 512), lambda b, j: (b * nb + j, 0)),
            pl.BlockSpec((1, 1, tb, 256), lambda b, j: (layer, b, j, 0)),
            pl.BlockSpec((1, 1, tb, 256), lambda b, j: (layer, b, j, 0)),
            pl.BlockSpec((1, 1, tb, D_I), lambda b, j: (layer, b, j, 0)),
            pl.BlockSpec((1, tb, 256), lambda b, j: (b, j, 0)),
            pl.BlockSpec((1, tb, D_I), lambda b, j: (b, j, 0)),
            vx_spec,
        ],
        out_shape=[
            jax.ShapeDtypeStruct((rows, 1024), BF16),
            jax.ShapeDtypeStruct((rows, 512), BF16),
            jax.ShapeDtypeStruct(kst.shape, F32),
            jax.ShapeDtypeStruct(vst.shape, F32),
            jax.ShapeDtypeStruct(kist.shape, F32),
            jax.ShapeDtypeStruct((bsz, t_len, 256), BF16),
            jax.ShapeDtypeStruct((bsz, t_len, D_I), BF16),
            vx_shape,
        ],
        input_output_aliases={9: 2, 10: 3, 11: 4},
        compiler_params=_cparams(("parallel", "parallel")),
        name="rope",
    )(proj, proj, proj, proj, proj, cq, sq, ci, si, kst, vst, kist)


def _mm(a, b):
    return jnp.dot(a.astype(BF16), b.astype(BF16), preferred_element_type=F32)


def _mm_nt(a, b):
    return lax.dot_general(a.astype(BF16), b.astype(BF16), (((1,), (1,)), ((), ())), preferred_element_type=F32)


def _mm_tn(a, b):
    return lax.dot_general(a.astype(BF16), b.astype(BF16), (((0,), (0,)), ((), ())), preferred_element_type=F32)


def _split(x):
    hi = x.astype(BF16)
    return hi, (x - hi.astype(F32)).astype(BF16)


def _mm3(a_hi, a_lo, b_hi, b_lo):
    def d(x, y):
        return jnp.dot(x, y, preferred_element_type=F32)
    return d(a_hi, b_hi) + (d(a_hi, b_lo) + d(a_lo, b_hi))


def _delta_kernel(alog_ref, dtb_ref, q_ref, k_ref, v_ref, z_ref, sm_ref, hq_ref, hk_ref, hv_ref, wq_ref, wk_ref,
                  wv_ref, og_ref, s0_ref, o_ref, sfin_ref, s_sc, xq_sc, xk_sc, xv_sc, u_sc, wq_sc, kd_sc,
                  qk_sc, ge_sc, *, c, tb, hb):
    hg = pl.program_id(1)
    tblk = pl.program_id(2)
    nc = tb // c
    n_st = int(math.log2(c))

    @pl.when(tblk == 0)
    def _():
        s_sc[...] = s0_ref[0]
        xq_sc[0:8, :] = hq_ref[0]
        xk_sc[0:8, :] = hk_ref[0]
        xv_sc[0:8, :] = hv_ref[0]

    @pl.when(tblk > 0)
    def _():
        xq_sc[0:8, :] = xq_sc[tb:tb + 8, :]
        xk_sc[0:8, :] = xk_sc[tb:tb + 8, :]
        xv_sc[0:8, :] = xv_sc[tb:tb + 8, :]

    xq_sc[8:8 + tb, :] = q_ref[...]
    xk_sc[8:8 + tb, :] = k_ref[...]
    xv_sc[8:8 + tb, :] = v_ref[...]

    lane = lax.broadcasted_iota(I32, (c, LANE), 1)
    row = lax.broadcasted_iota(I32, (c, LANE), 0)
    ri = lax.broadcasted_iota(I32, (c, c), 0)
    cj = lax.broadcasted_iota(I32, (c, c), 1)
    tri_incl = ri >= cj
    tri_strict = ri > cj
    eye = jnp.where(ri == cj, 1.0, 0.0).astype(F32)

    def cumsum_rows(x):
        s = 1
        while s < c:
            x = x + jnp.where(row >= s, pltpu.roll(x, s, 0), 0.0)
            s *= 2
        return x

    def conv_silu(x_sc, w_ref, r0, sl):
        win = x_sc[pl.ds(r0, c + 8), sl]
        acc = win[8:, :] * w_ref[SHORT_CONV - 1:SHORT_CONV, sl]
        for jj in range(SHORT_CONV - 1):
            sh = SHORT_CONV - 1 - jj
            acc = acc + pltpu.roll(win, sh, 0)[8:, :] * w_ref[jj:jj + 1, sl]
        return _silu(acc)

    def l2n(x):
        return x * lax.rsqrt(jnp.sum(x * x, axis=-1, keepdims=True) + EPS)

    def gates(ci, hh):
        h = hg * hb + hh
        sm = sm_ref[pl.ds(pl.multiple_of(ci * c, c), c), :]
        a_col = jnp.sum(jnp.where(lane == SM_A + h, sm, 0.0), axis=1, keepdims=True)
        b_col = jnp.sum(jnp.where(lane == SM_B + h, sm, 0.0), axis=1, keepdims=True)
        a_b = jnp.broadcast_to(a_col, (c, LANE)) + dtb_ref[h]
        softplus = jnp.maximum(a_b, 0.0) + jnp.log1p(jnp.exp(-jnp.abs(a_b)))
        g_b = -jnp.exp(jnp.full((c, LANE), alog_ref[h], F32)) * softplus
        return cumsum_rows(g_b), jnp.broadcast_to(_sigmoid(b_col), (c, LANE))

    def prep_stages(ci, slot):
        st = []

        def vpu():
            for hh in range(hb):
                r0 = pl.multiple_of(ci * c, c)
                sl = slice(hh * LANE, (hh + 1) * LANE)
                q = l2n(conv_silu(xq_sc, wq_ref, r0, sl)) * (DK_A ** -0.5)
                k = l2n(conv_silu(xk_sc, wk_ref, r0, sl))
                v = conv_silu(xv_sc, wv_ref, r0, sl)
                big_g, beta_b = gates(ci, hh)
                g_row = jnp.sum(eye * big_g[:, :c], axis=0, keepdims=True)
                edec = jnp.exp(jnp.minimum(big_g[:, :c] - g_row, 0.0))
                kb = k * beta_b
                eg = jnp.exp(big_g)
                g_last = big_g[c - 1:c, :]
                kd_sc[hh, slot] = k * jnp.exp(g_last - big_g)
                ge_sc[hh, slot] = jnp.broadcast_to(jnp.exp(g_last), (8, LANE))
                st.append(dict(hh=hh, q=q, k=k, kb=kb, edec=edec, qe=q * eg,
                               rhs=jnp.concatenate([v * beta_b, kb * eg], axis=1)))

        def kq():
            for d in st:
                kq_ = _mm_nt(jnp.concatenate([d["kb"], d["q"]], axis=0), d["k"])
                qk_sc[d["hh"], slot] = kq_[c:] * jnp.where(tri_incl, d["edec"], 0.0)
                m = -(kq_[:c] * jnp.where(tri_strict, d["edec"], 0.0))
                d["x"] = jnp.concatenate([m, eye], axis=0)

        def double():
            for d in st:
                x_hi, x_lo = _split(d["x"])
                y = _mm3(x_hi, x_lo, x_hi[:c], x_lo[:c])
                d["x"] = jnp.concatenate([y[:c], d["x"][c:] + y[c:]], axis=0)

        def uw():
            for d in st:
                p_hi, p_lo = _split(d["x"][c:])
                r_hi, r_lo = _split(d["rhs"])
                uw_ = _mm3(p_hi, p_lo, r_hi, r_lo)
                u_sc[d["hh"], slot] = uw_[:, :LANE]
                wq_sc[d["hh"], slot] = jnp.concatenate([uw_[:, LANE:], d["qe"]], axis=0)

        return [vpu, kq] + [double] * n_st + [uw]

    og = og_ref[...]

    def scan_stages(ci, slot):
        st = {}

        def first():
            st["s"] = [s_sc[hh] for hh in range(hb)]
            st["ws_qs"] = [_mm(wq_sc[hh, slot], st["s"][hh]) for hh in range(hb)]

        def state():
            st["v_new"] = [u_sc[hh, slot] - st["ws_qs"][hh][:c] for hh in range(hb)]
            for hh in range(hb):
                s_sc[hh] = ge_sc[hh, slot][0:1, :] * st["s"][hh] + _mm_tn(kd_sc[hh, slot], st["v_new"][hh])

        def out():
            r0 = pl.multiple_of(ci * c, c)
            for hh in range(hb):
                sl = slice(hh * LANE, (hh + 1) * LANE)
                o = st["ws_qs"][hh][c:] + _mm(qk_sc[hh, slot], st["v_new"][hh])
                o = o * lax.rsqrt(jnp.mean(o * o, axis=-1, keepdims=True) + EPS) * og
                o_ref[pl.ds(r0, c), sl] = (o * _silu(z_ref[pl.ds(r0, c), sl])).astype(BF16)

        return [first, state, out]

    def run(stages):
        for f in stages:
            f()

    def step(k, slot_scan):
        prep = prep_stages(k, 1 - slot_scan)
        scan = scan_stages(k - 1, slot_scan)
        run([prep[0], scan[0], prep[1], prep[2], scan[1], prep[3], prep[4], scan[2]] + prep[5:])

    run(prep_stages(0, 0))
    n_mid = nc - 1

    def pair(it, carry):
        step(1 + 2 * it, 0)
        step(2 + 2 * it, 1)
        return carry

    lax.fori_loop(0, n_mid // 2, pair, 0)
    if n_mid % 2:
        step(nc - 1, 0)
    run(scan_stages(nc - 1, (nc - 1) % 2))
    sfin_ref[0] = s_sc[...]


def _delta(proj, bsz, t_len, hist8, s0, conv_w, a_log, dt_bias, onorm_g, *, tb, hb):
    c = CHUNK if t_len % CHUNK == 0 else t_len
    nb = t_len // tb
    hw = hb * LANE
    rows = bsz * t_len

    def col(off):
        return lambda b, g, t: (b * nb + t, off // hw + g)

    def hcol(off):
        return lambda b, g, t: (b, 0, off // hw + g)

    def wcol(off):
        return lambda b, g, t: (0, off // hw + g)

    smem = pl.BlockSpec(memory_space=pltpu.SMEM)
    kern = functools.partial(_delta_kernel, c=c, tb=tb, hb=hb)
    return pl.pallas_call(
        kern,
        grid=(bsz, H_A // hb, nb),
        in_specs=[
            smem, smem,
            pl.BlockSpec((tb, hw), col(COL_QKV)),
            pl.BlockSpec((tb, hw), col(COL_QKV + 1024)),
            pl.BlockSpec((tb, hw), col(COL_QKV + 2048)),
            pl.BlockSpec((tb, hw), col(COL_ZA)),
            pl.BlockSpec((tb, LANE), lambda b, g, t: (b * nb + t, COL_SMALL // LANE)),
            pl.BlockSpec((1, 8, hw), hcol(0)),
            pl.BlockSpec((1, 8, hw), hcol(1024)),
            pl.BlockSpec((1, 8, hw), hcol(2048)),
            pl.BlockSpec((SHORT_CONV, hw), wcol(0)),
            pl.BlockSpec((SHORT_CONV, hw), wcol(1024)),
            pl.BlockSpec((SHORT_CONV, hw), wcol(2048)),
            pl.BlockSpec((1, LANE), lambda b, g, t: (0, 0)),
            pl.BlockSpec((1, hb, DK_A, DK_A), lambda b, g, t: (b, g, 0, 0)),
        ],
        out_specs=[
            pl.BlockSpec((tb, hw), lambda b, g, t: (b * nb + t, g)),
            pl.BlockSpec((1, hb, DK_A, DK_A), lambda b, g, t: (b, g, 0, 0)),
        ],
        out_shape=[
            jax.ShapeDtypeStruct((rows, 1024), BF16),
            jax.ShapeDtypeStruct((bsz, H_A, DK_A, DK_A), F32),
        ],
        scratch_shapes=[
            pltpu.VMEM((hb, DK_A, DK_A), F32),
            pltpu.VMEM((tb + 8, hw), F32),
            pltpu.VMEM((tb + 8, hw), F32),
            pltpu.VMEM((tb + 8, hw), F32),
            pltpu.VMEM((hb, 2, c, LANE), F32),
            pltpu.VMEM((hb, 2, 2 * c, LANE), F32),
            pltpu.VMEM((hb, 2, c, LANE), F32),
            pltpu.VMEM((hb, 2, c, c), F32),
            pltpu.VMEM((hb, 2, 8, LANE), F32),
        ],
        compiler_params=_cparams(("parallel", "parallel", "arbitrary")),
        name="delta",
    )(a_log, dt_bias, proj, proj, proj, proj, proj, hist8, hist8, hist8, conv_w, conv_w, conv_w, onorm_g, s0)


def _convb_kernel(glu_ref, halo_ref, hist_ref, z_ref, w_ref, bias_ref, lng_ref, lnb_ref, o_ref, tail_ref, xp_ref,
                  y_ref, *, tb):
    j = pl.program_id(1)

    @pl.when(j == 0)
    def _():
        xp_ref[0:HALO, :] = hist_ref[0]

    @pl.when(j > 0)
    def _():
        hl = halo_ref[...]
        xp_ref[0:HALO, :] = hl[:, :C_B] * _sigmoid(hl[:, C_B:])

    gl = glu_ref[...]
    xp_ref[HALO:HALO + tb, :] = gl[:, :C_B] * _sigmoid(gl[:, C_B:])
    tail_ref[0] = xp_ref[tb:tb + HALO, :]

    rc_n = min(tb, 64)
    cc_n = LANE
    base = HALO - (CONV_B - 1)
    win_n = rc_n + HALO
    for cc in range(C_B // cc_n):
        cs = slice(cc * cc_n, (cc + 1) * cc_n)
        for rc in range(tb // rc_n):
            win = xp_ref[rc * rc_n:rc * rc_n + win_n, cs]
            acc = jnp.zeros((rc_n, cc_n), F32)
            for ph in range(8):
                wph = win if ph == 0 else pltpu.roll(win, win_n - ph, 0)
                for a in range(win_n // 8):
                    k = 8 * a + ph - base
                    if 0 <= k < CONV_B and 8 * a + rc_n <= win_n:
                        acc = acc + w_ref[k:k + 1, cs] * wph[8 * a:8 * a + rc_n, :]
            y_ref[rc * rc_n:(rc + 1) * rc_n, cs] = acc

    y = y_ref[...] + bias_ref[...]
    mu = jnp.mean(y, axis=-1, keepdims=True)
    yc = y - mu
    var = jnp.mean(yc * yc, axis=-1, keepdims=True)
    yn = yc * lax.rsqrt(var + EPS) * lng_ref[...] + lnb_ref[...]
    o_ref[...] = (_silu(yn) * _silu(z_ref[...])).astype(BF16)


def _convb(proj, bsz, t_len, hist32, w32, bias, ln_g, ln_b, *, tb):
    nb = t_len // tb
    rows = bsz * t_len
    rb = tb // HALO
    vec = pl.BlockSpec((1, C_B), lambda b, j: (0, 0))
    kern = functools.partial(_convb_kernel, tb=tb)
    return pl.pallas_call(
        kern,
        grid=(bsz, nb),
        in_specs=[
            pl.BlockSpec((tb, 2 * C_B), lambda b, j: (b * nb + j, COL_GLU // (2 * C_B))),
            pl.BlockSpec((HALO, 2 * C_B), lambda b, j: (jnp.maximum((b * nb + j) * rb - 1, 0), COL_GLU // (2 * C_B))),
            pl.BlockSpec((1, HALO, C_B), lambda b, j: (b, 0, 0)),
            pl.BlockSpec((tb, C_B), lambda b, j: (b * nb + j, COL_ZB // C_B)),
            pl.BlockSpec((HALO, C_B), lambda b, j: (0, 0)),
            vec, vec, vec,
        ],
        out_specs=[
            pl.BlockSpec((tb, C_B), lambda b, j: (b * nb + j, 0)),
            pl.BlockSpec((1, HALO, C_B), lambda b, j: (b, 0, 0)),
        ],
        out_shape=[
            jax.ShapeDtypeStruct((rows, C_B), BF16),
            jax.ShapeDtypeStruct((bsz, HALO, C_B), F32),
        ],
        scratch_shapes=[pltpu.VMEM((HALO + tb, C_B), F32), pltpu.VMEM((tb, C_B), F32)],
        compiler_params=_cparams(("parallel", "arbitrary")),
        name="convb",
    )(proj, proj, hist32, proj, w32, bias, ln_g, ln_b)


SUB = 128
VT_ONES = 16
QK_AHEAD = 2
QK_AHEAD1 = QK_AHEAD + 1


def _tree(parts, op):
    parts = list(parts)
    while len(parts) > 1:
        nxt = [op(parts[n], parts[n + 1]) for n in range(0, len(parts) - 1, 2)]
        if len(parts) % 2:
            nxt.append(parts[-1])
        parts = nxt
    return parts[0]


def _fold8(x, op):
    return _tree([x[r:r + 8, :] for r in range(0, x.shape[0], 8)], op)


def _dsa_t_kernel(qi_ref, wt_ref, ki_ref, q_ref, k_ref, vt_ref, z_ref, o_ref, key_ref, khi_ref, klo_ref, bias_ref,
                  s_ref, p_ref, m_ref, acc_ref, *, tq, n_real, kb, n_keys, n_valid, pos0, topk):
    i = pl.program_id(1)
    nsub = kb // SUB
    last_pos = pos0 + (i + 1) * tq - 1
    lim_last = jnp.minimum((last_pos // CHUNK + 1) * CHUNK, n_valid)
    nkb = jnp.minimum((lim_last + kb - 1) // kb, n_keys // kb)
    qpos = pos0 + i * tq + lax.broadcasted_iota(I32, (1, tq), 1)
    lim = jnp.minimum(((qpos >> 6) + 1) << 6, n_valid)
    sub_iota = lax.broadcasted_iota(I32, (SUB, 1), 0)
    wt = wt_ref[0] * (H_I ** -0.5)
    nt = (((1,), (1,)), ((), ()))

    def p1(kblk, carry):
        for sb in range(nsub):
            r0 = pl.multiple_of(kblk * kb + sb * SUB, SUB)
            kis = ki_ref[0, pl.ds(r0, SUB), :]
            acc = jnp.zeros((SUB, tq), F32)
            for j in range(H_I):
                lg = lax.dot_general(kis, qi_ref[:, j * D_I:(j + 1) * D_I], nt, preferred_element_type=F32)
                acc = acc + wt[j:j + 1, :] * jnp.maximum(lg, 0.0)
            bits = lax.bitcast_convert_type(jnp.where((r0 + sub_iota) < lim, acc, NEG_INF), I32)
            key = bits ^ ((bits >> 31) & 0x7FFFFFFF)
            ss = slice(sb * SUB, (sb + 1) * SUB)
            key_ref[kblk, ss, :] = key
            khi_ref[kblk, ss, :] = (key >> 16).astype(I16)
            klo_ref[kblk, ss, :] = ((key & 0xFFFF) - 32768).astype(I16)
        return carry

    lax.fori_loop(0, nkb, p1, 0)

    def count(pred):
        def body(kblk, cnt):
            parts = [cnt]
            for sb in range(nsub):
                blk = key_ref[kblk, sb * SUB:(sb + 1) * SUB, :]
                parts.append(_fold8(jnp.where(pred(blk, kblk * kb + sb * SUB), 1.0, 0.0), jnp.add))
            return _tree(parts, jnp.add)

        cnt = lax.fori_loop(0, nkb, body, jnp.zeros((8, tq), F32))
        return jnp.sum(cnt, axis=0, keepdims=True)

    def count_ge(cand):
        return count(lambda blk, _: blk >= cand)

    one16 = jnp.ones((16, tq), I16)
    zero16 = jnp.zeros((16, tq), I16)

    def count16_ge(h_ref, cand_s):
        cb = jnp.broadcast_to(cand_s, (16, tq)).astype(I16)

        def body(kblk, cnt):
            parts = [cnt]
            for sb in range(nsub):
                blk = h_ref[kblk, sb * SUB:(sb + 1) * SUB, :]
                parts.append(_tree([jnp.where(blk[r:r + 16, :] >= cb, one16, zero16) for r in range(0, SUB, 16)],
                                   jnp.add))
            return _tree(parts, jnp.add)

        cnt = lax.fori_loop(0, nkb, body, zero16)
        return jnp.sum(cnt.astype(I32).astype(F32), axis=0, keepdims=True)

    def bisect16(h_ref):
        def step(p, t_u):
            cand_u = t_u | jnp.left_shift(jnp.int32(1), 15 - p)
            return jnp.where(count16_ge(h_ref, cand_u - 32768) >= topk, cand_u, t_u)

        return lax.fori_loop(0, 16, step, jnp.zeros((1, tq), I32))

    th_s = bisect16(khi_ref) - 32768
    th16 = jnp.broadcast_to(th_s, (16, tq)).astype(I16)

    def mask_low(kblk, carry):
        for sb in range(nsub):
            for r in range(0, SUB, 16):
                rs = slice(sb * SUB + r, sb * SUB + r + 16)
                hi = khi_ref[kblk, rs, :]
                other = jnp.where(hi > th16, jnp.int16(32767), jnp.int16(-32768))
                klo_ref[kblk, rs, :] = jnp.where(hi == th16, klo_ref[kblk, rs, :], other)
        return carry

    lax.fori_loop(0, nkb, mask_low, 0)
    thr = th_s * 65536 + bisect16(klo_ref)
    thr = jnp.maximum(thr, NEG_INF_KEY + 1)
    cnt_ge = count_ge(thr)

    real_q = lax.broadcasted_iota(I32, (1, tq), 1) < n_real

    @pl.when(jnp.max(jnp.where(real_q, cnt_ge, 0.0)) > topk)
    def _():
        need = topk - count_ge(thr + 1)
        nbits = max(n_keys - 1, 1).bit_length()

        def bis_idx(p, cm):
            cand = cm | jnp.left_shift(jnp.int32(1), nbits - 1 - p)
            below = count(lambda blk, k0: (blk == thr) & ((k0 + sub_iota) < cand))
            return jnp.where(below < need, cand, cm)

        cm = lax.fori_loop(0, nbits, bis_idx, jnp.zeros((1, tq), I32))

        def rewrite(kblk, carry):
            for sb in range(nsub):
                blk = key_ref[kblk, sb * SUB:(sb + 1) * SUB, :]
                drop = (blk == thr) & ((kblk * kb + sb * SUB + sub_iota) > cm)
                key_ref[kblk, sb * SUB:(sb + 1) * SUB, :] = jnp.where(drop, thr - 1, blk)
            return carry

        lax.fori_loop(0, nkb, rewrite, 0)

    m_ref[...] = jnp.full(m_ref.shape, NEG_INF, F32)
    acc_ref[...] = jnp.zeros(acc_ref.shape, F32)
    group = H_C // H_KV
    c_exp = (DH_C ** -0.5) * math.log2(math.e)
    va = DH_C + VT_ONES

    def p3(kblk, carry):
        k0 = pl.multiple_of(kblk * kb, kb)
        for sb in range(nsub):
            ss = slice(sb * SUB, (sb + 1) * SUB)
            bias_ref[ss, :] = jnp.where(key_ref[kblk, ss, :] >= thr, 0.0, NEG_INF)

        def logits(h):
            g = h // group
            mx = []
            for sb in range(nsub):
                ss = slice(sb * SUB, (sb + 1) * SUB)
                ks = k_ref[0, pl.ds(k0 + sb * SUB, SUB), g * DH_C:(g + 1) * DH_C]
                s = lax.dot_general(ks, q_ref[:, h * DH_C:(h + 1) * DH_C], nt, preferred_element_type=F32)
                s = s + bias_ref[ss, :]
                s_ref[h % QK_AHEAD1, ss, :] = s
                mx.append(_fold8(s, jnp.maximum))
            return jnp.max(_tree(mx, jnp.maximum), axis=0, keepdims=True)

        m_q = [logits(h) for h in range(QK_AHEAD)]
        for h in range(H_C):
            if h + QK_AHEAD < H_C:
                m_q.append(logits(h + QK_AHEAD))
            m_cur = m_q[h]
            g = h // group
            m_prev = m_ref[h][0:1, :]
            m_new = jnp.maximum(m_prev, m_cur)
            m_safe = jnp.where(m_new == NEG_INF, 0.0, m_new)
            alpha = jnp.exp2((m_prev - m_safe) * c_exp)
            for sb in range(nsub):
                ss = slice(sb * SUB, (sb + 1) * SUB)
                p_ref[ss, :] = jnp.exp2((s_ref[h % QK_AHEAD1, ss, :] - m_safe) * c_exp).astype(BF16)
            pv = jnp.dot(vt_ref[0, kblk, g * va:(g + 1) * va, :], p_ref[...], preferred_element_type=F32)
            acc_ref[h] = alpha * acc_ref[h] + pv
            m_ref[h] = jnp.broadcast_to(m_new, (8, tq))
        return carry

    lax.fori_loop(0, nkb, p3, 0)

    for h in range(H_C):
        hs = slice(h * DH_C, (h + 1) * DH_C)
        o = (acc_ref[h, 0:DH_C, :] / acc_ref[h, DH_C:DH_C + 1, :]).T
        o_ref[:, hs] = (o * _silu(z_ref[:, hs])).astype(BF16)


def _dsa_t(z_arr, z_col, qi_r, q_r, wi_t, ki_all, k_all, vt, bsz, t_len, n_valid, pos0, *, tq, n_real, kb):
    n_keys = k_all.shape[1]
    nq = t_len // tq
    rows = bsz * t_len
    topk = min(TOPK, n_valid // 4)
    kern = functools.partial(_dsa_t_kernel, tq=tq, n_real=n_real, kb=kb, n_keys=n_keys, n_valid=n_valid, pos0=pos0,
                             topk=topk)
    return pl.pallas_call(
        kern,
        grid=(bsz, nq),
        in_specs=[
            pl.BlockSpec((tq, 512), lambda b, i: (b * nq + i, 0)),
            pl.BlockSpec((1, H_I, tq), lambda b, i: (b * nq + i, 0, 0)),
            pl.BlockSpec((1, n_keys, D_I), lambda b, i: (b, 0, 0)),
            pl.BlockSpec((tq, 1024), lambda b, i: (b * nq + i, 0)),
            pl.BlockSpec((1, n_keys, 256), lambda b, i: (b, 0, 0)),
            pl.BlockSpec((1, n_keys // kb, H_KV * (DH_C + VT_ONES), kb), lambda b, i: (b, 0, 0, 0)),
            pl.BlockSpec((tq, 1024), lambda b, i: (b * nq + i, z_col)),
        ],
        out_specs=pl.BlockSpec((tq, 1024), lambda b, i: (b * nq + i, 0)),
        out_shape=jax.ShapeDtypeStruct((rows, 1024), BF16),
        scratch_shapes=[
            pltpu.VMEM((n_keys // kb, kb, tq), I32),
            pltpu.VMEM((n_keys // kb, kb, tq), I16),
            pltpu.VMEM((n_keys // kb, kb, tq), I16),
            pltpu.VMEM((kb, tq), F32),
            pltpu.VMEM((QK_AHEAD1, kb, tq), F32),
            pltpu.VMEM((kb, tq), BF16),
            pltpu.VMEM((H_C, 8, tq), F32),
            pltpu.VMEM((H_C, DH_C + VT_ONES, tq), F32),
        ],
        compiler_params=_cparams(("parallel", "arbitrary")),
        name="dsa_t",
    )(qi_r, wi_t, ki_all, q_r, k_all, vt, z_arr)


def _merge_kernel(oa_ref, ub_ref, oc_ref, gate_ref, x_ref, woa_ref, wpw_ref, woc_ref, wout_ref, fg_ref, xo_ref,
                  *rest, final):
    ya = jnp.dot(oa_ref[...], woa_ref[0], preferred_element_type=F32)
    mix = _sigmoid(gate_ref[:, 0:D_MODEL]) * ya
    yb = jnp.dot(ub_ref[...], wpw_ref[0], preferred_element_type=F32)
    mix = mix + _sigmoid(gate_ref[:, D_MODEL:2 * D_MODEL]) * yb
    yc = jnp.dot(oc_ref[...], woc_ref[0], preferred_element_type=F32)
    mix = mix + _sigmoid(gate_ref[:, 2 * D_MODEL:3 * D_MODEL]) * yc
    xn = x_ref[...] + jnp.dot(mix.astype(BF16), wout_ref[0], preferred_element_type=F32)
    xo_ref[...] = xn
    if final:
        ms = jnp.mean(xn * xn, axis=-1, keepdims=True)
        rest[0][...] = xn * lax.rsqrt(ms + EPS) * fg_ref[...]


def _merge(o_a, u_b, o_c, proj, x2d, w_oa, w_pw, w_oc, w_out, final_g, layer, final):
    rows = x2d.shape[0]
    tm = _pick(rows, (512, 256, 128, 64, 32, 16, 8))
    act = pl.BlockSpec((tm, D_MODEL), lambda i: (i, 0))
    wsp = pl.BlockSpec((1, D_MODEL, D_MODEL), lambda i: (layer, 0, 0))
    out_specs = [act]
    out_shape = [jax.ShapeDtypeStruct((rows, D_MODEL), F32)]
    if final:
        out_specs.append(act)
        out_shape.append(jax.ShapeDtypeStruct((rows, D_MODEL), F32))
    res = pl.pallas_call(
        functools.partial(_merge_kernel, final=final),
        grid=(rows // tm,),
        in_specs=[act, act, act, pl.BlockSpec((tm, 3 * D_MODEL), lambda i: (i, COL_GATE // (3 * D_MODEL))), act,
                  wsp, wsp, wsp, wsp, pl.BlockSpec((1, D_MODEL), lambda i: (0, 0))],
        out_specs=out_specs,
        out_shape=out_shape,
        compiler_params=_cparams(("parallel",)),
        name="merge",
    )(o_a, u_b, o_c, proj, x2d, w_oa, w_pw, w_oc, w_out, final_g)
    return res if final else (res[0], None)


def _permute_w_in(w_in):
    offs = [0]
    for s in IN_SIZES:
        offs.append(offs[-1] + s)
    (qkv, a_in, b_in, z_a, glu, z_b, q_c, k_c, v_c, qi, ki, wi, z_c, gate) = [
        w_in[..., offs[n]:offs[n + 1]] for n in range(len(IN_SIZES))]

    def zeros(n):
        return jnp.zeros(w_in.shape[:-1] + (n,), BF16)

    parts = [qkv, gate, glu, z_a, z_b, q_c, z_c, qi, k_c, v_c, a_in, b_in, wi, zeros(SM_KI - SM_WI - H_I), ki]
    used = sum(p.shape[-1] for p in parts)
    return jnp.concatenate([p.astype(BF16) for p in parts] + [zeros(NP - used)], axis=-1)


def _stream_layer(x2d, layer, final, cfg, state, prm, stacks):
    bsz, t_len, pos0 = cfg["bsz"], cfg["t_len"], cfg["pos0"]
    proj = _inproj(x2d, prm["norm_g"], prm["w_in"], layer)

    o_a, delta_new = _delta(proj, bsz, t_len, state["conv_a8"], state["delta"], prm["conv_a_w"][layer],
                            prm["a_log"][layer], prm["dt_bias"][layer], prm["onorm_a_g"][layer][None, :],
                            tb=cfg["delta_tb"], hb=cfg["delta_hb"])
    conv_a_new = proj.reshape(bsz, t_len, NP)[:, t_len - (SHORT_CONV - 1):, COL_QKV:COL_QKV + 3 * D_MODEL]

    u_b, tail_b = _convb(proj, bsz, t_len, state["conv_b32"], prm["conv_b_w32"][layer],
                         prm["conv_b_bias"][layer][None, :], prm["ln_b_g"][layer][None, :],
                         prm["ln_b_b"][layer][None, :], tb=cfg["convb_tb"])
    conv_b_new = tail_b[:, HALO - (CONV_B - 1):, :]

    n_valid = pos0 + t_len
    kb = cfg["dsa_kb"]
    tq = cfg["dsa_tq"]
    n_keys = -(-n_valid // kb) * kb
    direct = tq == cfg["dsa_tq_real"] and pos0 == 0 and n_keys == n_valid
    q_r, qi_r, kst, vst, kist, k_bf, ki_bf, v_x = _rope(
        proj, stacks, layer, bsz, t_len, pos0, tb=kb if direct else _pick(t_len, (512, 256, 128, 64, 32)),
        transpose_v=direct)
    wi = proj[:, COL_SMALL + SM_WI:COL_SMALL + SM_WI + H_I]
    if direct:
        wi_t = wi.reshape(bsz * t_len // tq, tq, H_I).transpose(0, 2, 1)
        o_c = _dsa_t(proj, COL_ZC // 1024, qi_r, q_r, wi_t, ki_bf, k_bf, v_x, bsz, t_len, n_valid, pos0, tq=tq,
                     n_real=tq, kb=kb)
    else:
        def with_past(past, new, width):
            parts = [past, new] if pos0 else [new]
            if n_keys > n_valid:
                parts.append(jnp.zeros((bsz, n_keys - n_valid, width), BF16))
            return jnp.concatenate(parts, axis=1) if len(parts) > 1 else new

        def pad_q(a):
            a = a.reshape(bsz, t_len, a.shape[-1])
            return jnp.pad(a, ((0, 0), (0, tq - t_len), (0, 0))).reshape(bsz * tq, a.shape[-1])

        k_all = with_past(state.get("past_k"), k_bf, 256)
        ki_all = with_past(state.get("past_ki"), ki_bf, D_I)
        vt = with_past(state.get("past_v"), v_x, 256).reshape(bsz, n_keys // kb, kb, H_KV, DH_C).transpose(0, 1, 3, 4, 2)
        vt = jnp.concatenate([vt, jnp.ones((bsz, n_keys // kb, H_KV, VT_ONES, kb), BF16)], axis=3)
        vt = vt.reshape(bsz, n_keys // kb, H_KV * (DH_C + VT_ONES), kb)
        wi_t = pad_q(wi).reshape(bsz, tq, H_I).transpose(0, 2, 1)
        o_c = _dsa_t(pad_q(proj[:, COL_ZC:COL_ZC + 1024]), 0, pad_q(qi_r), pad_q(q_r), wi_t, ki_all, k_all, vt, bsz, tq,
                     n_valid, pos0, tq=tq, n_real=t_len, kb=kb)
        o_c = o_c.reshape(bsz, tq, 1024)[:, :t_len].reshape(bsz * t_len, 1024)

    x_new, y_fin = _merge(o_a, u_b, o_c, proj, x2d, prm["w_o_a"], prm["w_pw2_b"], prm["w_o_c"], prm["w_out"],
                          prm["final_norm_g"], layer, final)
    return x_new, y_fin, (conv_a_new, delta_new, conv_b_new), (kst, vst, kist)


def _run(x_prompt, x_sample, cache_k, cache_v, cache_idx_k, state_conv_a, state_delta, state_conv_b, prm, cfg_p,
         cfg_s):
    n_layers = prm["norm_g"].shape[0]
    bp, tp = cfg_p["bsz"], cfg_p["t_len"]
    bs, ts = cfg_s["bsz"], cfg_s["t_len"]
    past = cfg_s["pos0"]
    xp = x_prompt.reshape(bp * tp, D_MODEL)
    xs = x_sample.reshape(bs * ts, D_MODEL)
    zero_state = {
        "conv_a8": jnp.zeros((bp, 8, 3 * D_MODEL), F32),
        "delta": jnp.zeros((bp, H_A, DK_A, DK_A), F32),
        "conv_b32": jnp.zeros((bp, HALO, C_B), F32),
    }

    def empty_stacks(bsz, t_len):
        return (jnp.zeros((n_layers, bsz, t_len, 256), F32), jnp.zeros((n_layers, bsz, t_len, 256), F32),
                jnp.zeros((n_layers, bsz, t_len, D_I), F32))

    stk_p, stk_s = empty_stacks(bp, tp), empty_stacks(bs, ts)
    st_p, st_s = [], []
    yp = ys = None
    for layer in range(n_layers):
        final = layer == n_layers - 1
        xp, yp, sp, stk_p = _stream_layer(xp, layer, final, cfg_p, zero_state, prm, stk_p)
        s_state = {
            "conv_a8": jnp.pad(state_conv_a[layer], ((0, 0), (8 - (SHORT_CONV - 1), 0), (0, 0))),
            "delta": state_delta[layer],
            "conv_b32": jnp.pad(state_conv_b[layer], ((0, 0), (HALO - (CONV_B - 1), 0), (0, 0))),
            "past_k": cache_k[layer].reshape(bs, past, 256).astype(BF16),
            "past_v": cache_v[layer].reshape(bs, past, 256).astype(BF16),
            "past_ki": cache_idx_k[layer].astype(BF16),
        }
        xs, ys, ss, stk_s = _stream_layer(xs, layer, final, cfg_s, s_state, prm, stk_s)
        st_p.append(sp)
        st_s.append(ss)

    def outputs(stk, states, bsz, t_len):
        kst, vst, kist = stk
        return (kst.reshape(n_layers, bsz, t_len, H_KV, DH_C), vst.reshape(n_layers, bsz, t_len, H_KV, DH_C), kist
                ) + tuple(jnp.stack([st[n] for st in states], axis=0) for n in range(3))

    return ((yp.reshape(bp, tp, D_MODEL), ys.reshape(bs, ts, D_MODEL))
            + outputs(stk_p, st_p, bp, tp) + outputs(stk_s, st_s, bs, ts))


def _prep_params(norm_g, w_in, conv_a_w, a_log, dt_bias, onorm_a_g, w_o_a, conv_b_w, conv_b_bias, ln_b_g, ln_b_b,
                 w_pw2_b, w_o_c, w_out, final_norm_g):
    return {
        "norm_g": norm_g[:, None, :], "w_in": _permute_w_in(w_in), "conv_a_w": conv_a_w, "a_log": a_log, "dt_bias": dt_bias,
        "onorm_a_g": onorm_a_g, "w_o_a": w_o_a.astype(BF16),
        "conv_b_w32": jnp.pad(conv_b_w, ((0, 0), (0, HALO - CONV_B), (0, 0))),
        "conv_b_bias": conv_b_bias, "ln_b_g": ln_b_g, "ln_b_b": ln_b_b, "w_pw2_b": w_pw2_b.astype(BF16),
        "w_o_c": w_o_c.astype(BF16), "w_out": w_out.astype(BF16), "final_norm_g": final_norm_g[None, :],
    }


def _stream_cfg(bsz, t_len, pos0):
    c = CHUNK if t_len % CHUNK == 0 else t_len
    n_valid = pos0 + t_len
    if t_len % 256 == 0:
        tq, kb = 256, 512
    else:
        tq = -(-t_len // LANE) * LANE
        kb = 384 if n_valid > 384 else LANE
    delta_tb = _pick(t_len, (512, 256, 128, 64, t_len))
    if delta_tb % c:
        delta_tb = t_len
    return {
        "bsz": bsz, "t_len": t_len, "pos0": pos0,
        "delta_tb": delta_tb, "delta_hb": H_A,
        "convb_tb": _pick(t_len, (256, 128, 64, 32)),
        "dsa_tq": tq, "dsa_tq_real": min(tq, t_len), "dsa_kb": kb,
    }


def kernel(x_prompt, x_sample, cache_k, cache_v, cache_idx_k, state_conv_a, state_delta, state_conv_b, norm_g, w_in,
           conv_a_w, a_log, dt_bias, onorm_a_g, w_o_a, conv_b_w, conv_b_bias, ln_b_g, ln_b_b, w_pw2_b, w_o_c, w_out,
           final_norm_g):
    prm = _prep_params(norm_g, w_in, conv_a_w, a_log, dt_bias, onorm_a_g, w_o_a, conv_b_w, conv_b_bias, ln_b_g,
                       ln_b_b, w_pw2_b, w_o_c, w_out, final_norm_g)
    cfg_p = _stream_cfg(x_prompt.shape[0], x_prompt.shape[1], 0)
    cfg_s = _stream_cfg(x_sample.shape[0], x_sample.shape[1], cache_k.shape[2])
    return _run(x_prompt, x_sample, cache_k, cache_v, cache_idx_k, state_conv_a, state_delta, state_conv_b, prm,
                cfg_p, cfg_s)
```

```python
import functools
import math

import jax
import jax.numpy as jnp
from jax import lax
from jax.experimental import pallas as pl
from jax.experimental.pallas import tpu as pltpu

F32 = jnp.float32
BF16 = jnp.bfloat16
I32 = jnp.int32
I16 = jnp.int16

D_MODEL = 1024
CHUNK = 64
H_A = 8
DK_A = 128
SHORT_CONV = 4
C_B = 1024
CONV_B = 31
H_C = 8
DH_C = 128
H_KV = 2
H_I = 8
D_I = 64
TOPK = 256
ROPE_THETA = 500000.0
ROPE_FRACTION = 4
EPS = 1e-6
IN_SIZES = (3 * D_MODEL, H_A, H_A, D_MODEL, 2 * C_B, C_B, H_C * DH_C, H_KV * DH_C, H_KV * DH_C, H_I * D_I, D_I, H_I,
            H_C * DH_C, 3 * D_MODEL)

LANE = 128
HALO = 32

COL_QKV = 0
COL_GATE = 3072
COL_GLU = 6144
COL_ZA = 8192
COL_ZB = 9216
COL_QC = 10240
COL_ZC = 11264
COL_QI = 12288
COL_KC = 12800
COL_VC = 13056
COL_SMALL = 13312
NP = 13824
SM_A, SM_B, SM_WI, SM_KI = 0, 8, 16, 64

NEG_INF = float("-inf")
NEG_INF_KEY = -2139095041

VMEM_LIMIT = 56 * 1024 * 1024
Q_SCALE = (DH_C ** -0.5) * math.log2(math.e)


def _cparams(sem):
    return pltpu.CompilerParams(dimension_semantics=sem, vmem_limit_bytes=VMEM_LIMIT)


def _pick(n, cands):
    for c in cands:
        if n % c == 0:
            return c
    raise ValueError(f"no block size in {cands} divides {n}")


def _sigmoid(x):
    return jax.nn.sigmoid(x)


def _silu(x):
    return x * jax.nn.sigmoid(x)


def _inproj_kernel(x_ref, g_ref, w_ref, o_ref, xn_ref):
    @pl.when(pl.program_id(1) == 0)
    def _():
        x = x_ref[...]
        ms = jnp.mean(x * x, axis=-1, keepdims=True)
        xn_ref[...] = (x * lax.rsqrt(ms + EPS) * g_ref[0]).astype(BF16)

    o_ref[...] = jnp.dot(xn_ref[...], w_ref[0], preferred_element_type=F32)


def _inproj(x2d, g_all, w_all, layer):
    rows = x2d.shape[0]
    tm = _pick(rows, (2048, 1024, 512, 256, 128, 64, 32, 16, 8))
    tn = 768
    return pl.pallas_call(
        _inproj_kernel,
        grid=(rows // tm, NP // tn),
        in_specs=[
            pl.BlockSpec((tm, D_MODEL), lambda i, j: (i, 0)),
            pl.BlockSpec((1, 1, D_MODEL), lambda i, j: (layer, 0, 0)),
            pl.BlockSpec((1, D_MODEL, tn), lambda i, j: (layer, 0, j)),
        ],
        out_specs=pl.BlockSpec((tm, tn), lambda i, j: (i, j)),
        out_shape=jax.ShapeDtypeStruct((rows, NP), F32),
        scratch_shapes=[pltpu.VMEM((tm, D_MODEL), BF16)],
        compiler_params=_cparams(("parallel", "arbitrary")),
        name="inproj",
    )(x2d, g_all, w_all)


def _rope_kernel(q_ref, k_ref, v_ref, qi_ref, sm_ref, cq_ref, sq_ref, ci_ref, si_ref, kst_in, vst_in, kist_in,
                 qo_ref, qio_ref, kst_ref, vst_ref, kist_ref, kbf_ref, kibf_ref, vx_ref, *, tq_t):
    del kst_in, vst_in, kist_in
    tb = q_ref.shape[0]
    lane = lax.broadcasted_iota(I32, (tb, LANE), 1)
    l64 = lane & 63
    cq, sq, ci, si = cq_ref[...], sq_ref[...], ci_ref[...], si_ref[...]

    def rot_qk(x):
        sw = jnp.where(lane < 16, pltpu.roll(x, LANE - 16, 1), pltpu.roll(x, 16, 1))
        return jnp.where(lane < 32, x * cq + sw * sq, x)

    def rot_idx(x):
        sw = jnp.where(l64 < 8, pltpu.roll(x, LANE - 8, 1), pltpu.roll(x, 8, 1))
        return jnp.where(l64 < 16, x * ci + sw * si, x)

    def put_q(o_ref, h, x):
        if tq_t:
            for n in range(tb // tq_t):
                o_ref[n, h * LANE:(h + 1) * LANE, :] = x[n * tq_t:(n + 1) * tq_t, :].T.astype(BF16)
        else:
            o_ref[:, h * LANE:(h + 1) * LANE] = x.astype(BF16)

    for h in range(H_C):
        put_q(qo_ref, h, rot_qk(q_ref[:, h * LANE:(h + 1) * LANE]))
    va = DH_C + VT_ONES
    for h in range(H_KV):
        sl = slice(h * LANE, (h + 1) * LANE)
        kr = rot_qk(k_ref[:, sl])
        kst_ref[0, 0, :, sl] = kr
        kbf_ref[0, :, sl] = kr.astype(BF16)
        v = v_ref[:, sl]
        vst_ref[0, 0, :, sl] = v
        if tq_t:
            vx_ref[0, 0, h * va:h * va + DH_C, :] = v.T.astype(BF16)
            vx_ref[0, 0, h * va + DH_C:(h + 1) * va, :] = jnp.ones((VT_ONES, tb), BF16)
        else:
            vx_ref[0, :, sl] = v.astype(BF16)
    for h in range(H_I * D_I // LANE):
        put_q(qio_ref, h, rot_idx(qi_ref[:, h * LANE:(h + 1) * LANE]) * (D_I ** -0.5))
    ki = rot_idx(sm_ref[...])[:, SM_KI:SM_KI + D_I]
    kist_ref[0, 0] = ki
    kibf_ref[0] = ki.astype(BF16)


def _rope_tables(t_len, pos0):
    pos = (pos0 + jnp.arange(t_len)).astype(F32)

    def table(d):
        rot = d // ROPE_FRACTION
        half = rot // 2
        inv = ROPE_THETA ** (-jnp.arange(half, dtype=F32) * 2.0 / rot)
        ang = pos[:, None] * inv[None, :]
        cos, sin = jnp.cos(ang), jnp.sin(ang)
        c = jnp.concatenate([cos, cos, jnp.ones((t_len, d - rot), F32)], axis=1)
        s = jnp.concatenate([-sin, sin, jnp.zeros((t_len, d - rot), F32)], axis=1)
        return jnp.tile(c, (1, LANE // d)), jnp.tile(s, (1, LANE // d))

    cq, sq = table(DH_C)
    ci, si = table(D_I)
    return cq, sq, ci, si


def _rope(proj, stacks, layer, bsz, t_len, pos0, *, tb, tq_t):
    nb = t_len // tb
    cq, sq, ci, si = _rope_tables(t_len, pos0)
    rows = bsz * t_len
    kst, vst, kist = stacks
    tab = pl.BlockSpec((tb, LANE), lambda b, j: (j, 0))
    anyspec = pl.BlockSpec(memory_space=pl.ANY)
    va2 = H_KV * (DH_C + VT_ONES)
    if tq_t:
        nt = tb // tq_t
        vx_spec = pl.BlockSpec((1, 1, va2, tb), lambda b, j: (b, j, 0, 0))
        vx_shape = jax.ShapeDtypeStruct((bsz, nb, va2, tb), BF16)
        q_spec = pl.BlockSpec((nt, 1024, tq_t), lambda b, j: (b * nb + j, 0, 0))
        q_shape = jax.ShapeDtypeStruct((rows // tq_t, 1024, tq_t), BF16)
        qi_spec = pl.BlockSpec((nt, 512, tq_t), lambda b, j: (b * nb + j, 0, 0))
        qi_shape = jax.ShapeDtypeStruct((rows // tq_t, 512, tq_t), BF16)
    else:
        vx_spec = pl.BlockSpec((1, tb, 256), lambda b, j: (b, j, 0))
        vx_shape = jax.ShapeDtypeStruct((bsz, t_len, 256), BF16)
        q_spec = pl.BlockSpec((tb, 1024), lambda b, j: (b * nb + j, 0))
        q_shape = jax.ShapeDtypeStruct((rows, 1024), BF16)
        qi_spec = pl.BlockSpec((tb, 512), lambda b, j: (b * nb + j, 0))
        qi_shape = jax.ShapeDtypeStruct((rows, 512), BF16)
    return pl.pallas_call(
        functools.partial(_rope_kernel, tq_t=tq_t),
        grid=(bsz, nb),
        in_specs=[
            pl.BlockSpec((tb, 1024), lambda b, j: (b * nb + j, COL_QC // 1024)),
            pl.BlockSpec((tb, 256), lambda b, j: (b * nb + j, COL_KC // 256)),
            pl.BlockSpec((tb, 256), lambda b, j: (b * nb + j, COL_VC // 256)),
            pl.BlockSpec((tb, 512), lambda b, j: (b * nb + j, COL_QI // 512)),
            pl.BlockSpec((tb, LANE), lambda b, j: (b * nb + j, COL_SMALL // LANE)),
            tab, tab, tab, tab,
            anyspec, anyspec, anyspec,
        ],
        out_specs=[
            q_spec,
            qi_spec,
            pl.BlockSpec((1, 1, tb, 256), lambda b, j: (layer, b, j, 0)),
            pl.BlockSpec((1, 1, tb, 256), lambda b, j: (layer, b, j, 0)),
            pl.BlockSpec((1, 1, tb, D_I), lambda b, j: (layer, b, j, 0)),
            pl.BlockSpec((1, tb, 256), lambda b, j: (b, j, 0)),
            pl.BlockSpec((1, tb, D_I), lambda b, j: (b, j, 0)),
            vx_spec,
        ],
        out_shape=[
            q_shape,
            qi_shape,
            jax.ShapeDtypeStruct(kst.shape, F32),
            jax.ShapeDtypeStruct(vst.shape, F32),
            jax.ShapeDtypeStruct(kist.shape, F32),
            jax.ShapeDtypeStruct((bsz, t_len, 256), BF16),
            jax.ShapeDtypeStruct((bsz, t_len, D_I), BF16),
            vx_shape,
        ],
        input_output_aliases={9: 2, 10: 3, 11: 4},
        compiler_params=_cparams(("parallel", "parallel")),
        name="rope",
    )(proj, proj, proj, proj, proj, cq, sq, ci, si, kst, vst, kist)


def _mm(a, b):
    return jnp.dot(a.astype(BF16), b.astype(BF16), preferred_element_type=F32)


def _mm_nt(a, b):
    return lax.dot_general(a.astype(BF16), b.astype(BF16), (((1,), (1,)), ((), ())), preferred_element_type=F32)


def _mm_tn(a, b):
    return lax.dot_general(a.astype(BF16), b.astype(BF16), (((0,), (0,)), ((), ())), preferred_element_type=F32)


def _split(x):
    hi = x.astype(BF16).astype(F32)
    return hi, x - hi


def _mm3(a_hi, a_lo, b_hi, b_lo):
    lhs = jnp.concatenate([a_hi, a_hi, a_lo], axis=1).astype(BF16)
    rhs = jnp.concatenate([b_hi, b_lo, b_hi], axis=0).astype(BF16)
    return jnp.dot(lhs, rhs, preferred_element_type=F32)


def _delta_kernel(alog_ref, dtb_ref, q_ref, k_ref, v_ref, z_ref, sm_ref, hq_ref, hk_ref, hv_ref, wq_ref, wk_ref,
                  wv_ref, og_ref, s0_ref, o_ref, sfin_ref, s_sc, xq_sc, xk_sc, xv_sc, u_sc, wq_sc, kd_sc,
                  qk_sc, ge_sc, *, c, tb, hb, layer):
    hg = pl.program_id(1)
    tblk = pl.program_id(2)
    nc = tb // c
    n_st = int(math.log2(c))

    @pl.when(tblk == 0)
    def _():
        s_sc[...] = s0_ref[0, 0]
        xq_sc[0:8, :] = hq_ref[0, 0]
        xk_sc[0:8, :] = hk_ref[0, 0]
        xv_sc[0:8, :] = hv_ref[0, 0]

    @pl.when(tblk > 0)
    def _():
        xq_sc[0:8, :] = xq_sc[tb:tb + 8, :]
        xk_sc[0:8, :] = xk_sc[tb:tb + 8, :]
        xv_sc[0:8, :] = xv_sc[tb:tb + 8, :]

    xq_sc[8:8 + tb, :] = q_ref[...]
    xk_sc[8:8 + tb, :] = k_ref[...]
    xv_sc[8:8 + tb, :] = v_ref[...]

    lane = lax.broadcasted_iota(I32, (c, LANE), 1)
    row = lax.broadcasted_iota(I32, (c, LANE), 0)
    ri = lax.broadcasted_iota(I32, (c, c), 0)
    cj = lax.broadcasted_iota(I32, (c, c), 1)
    tri_incl = ri >= cj
    tri_strict = ri > cj
    eye = jnp.where(ri == cj, 1.0, 0.0).astype(F32)

    def cumsum_rows(x):
        s = 1
        while s < c:
            x = x + jnp.where(row >= s, pltpu.roll(x, s, 0), 0.0)
            s *= 2
        return x

    def conv_silu(x_sc, w_ref, r0, sl):
        win = x_sc[pl.ds(r0, c + 8), sl]
        acc = win[8:, :] * w_ref[0, SHORT_CONV - 1:SHORT_CONV, sl]
        for jj in range(SHORT_CONV - 1):
            sh = SHORT_CONV - 1 - jj
            acc = acc + pltpu.roll(win, sh, 0)[8:, :] * w_ref[0, jj:jj + 1, sl]
        return _silu(acc)

    def l2n(x):
        return x * lax.rsqrt(jnp.sum(x * x, axis=-1, keepdims=True) + EPS)

    def gates(ci, hh):
        h = hg * hb + hh
        sm = sm_ref[pl.ds(pl.multiple_of(ci * c, c), c), :]
        a_col = jnp.sum(jnp.where(lane == SM_A + h, sm, 0.0), axis=1, keepdims=True)
        b_col = jnp.sum(jnp.where(lane == SM_B + h, sm, 0.0), axis=1, keepdims=True)
        a_b = jnp.broadcast_to(a_col, (c, LANE)) + dtb_ref[layer, h]
        softplus = jnp.maximum(a_b, 0.0) + jnp.log1p(jnp.exp(-jnp.abs(a_b)))
        g_b = -jnp.exp(jnp.full((c, LANE), alog_ref[layer, h], F32)) * softplus
        return cumsum_rows(g_b), jnp.broadcast_to(_sigmoid(b_col), (c, LANE))

    def prep_stages(ci, slot):
        st = []

        def vpu():
            for hh in range(hb):
                r0 = pl.multiple_of(ci * c, c)
                sl = slice(hh * LANE, (hh + 1) * LANE)
                q = l2n(conv_silu(xq_sc, wq_ref, r0, sl)) * (DK_A ** -0.5)
                k = l2n(conv_silu(xk_sc, wk_ref, r0, sl))
                v = conv_silu(xv_sc, wv_ref, r0, sl)
                big_g, beta_b = gates(ci, hh)
                g_row = jnp.sum(eye * big_g[:, :c], axis=0, keepdims=True)
                edec = jnp.exp(jnp.minimum(big_g[:, :c] - g_row, 0.0))
                kb = k * beta_b
                eg = jnp.exp(big_g)
                g_last = big_g[c - 1:c, :]
                kd_sc[hh, slot] = k * jnp.exp(g_last - big_g)
                ge_sc[hh, slot] = jnp.broadcast_to(jnp.exp(g_last), (8, LANE))
                st.append(dict(hh=hh, q=q, k=k, kb=kb, edec=edec, qe=q * eg,
                               rhs=jnp.concatenate([v * beta_b, kb * eg], axis=1)))

        def kq():
            for d in st:
                kq_ = _mm_nt(jnp.concatenate([d["kb"], d["q"]], axis=0), d["k"])
                qk_sc[d["hh"], slot] = kq_[c:] * jnp.where(tri_incl, d["edec"], 0.0)
                m = -(kq_[:c] * jnp.where(tri_strict, d["edec"], 0.0))
                d["x"] = jnp.concatenate([m, eye], axis=0)

        def double():
            for d in st:
                x_hi, x_lo = _split(d["x"])
                y = _mm3(x_hi, x_lo, x_hi[:c], x_lo[:c])
                d["x"] = jnp.concatenate([y[:c], d["x"][c:] + y[c:]], axis=0)

        def uw():
            for d in st:
                p_hi, p_lo = _split(d["x"][c:])
                r_hi, r_lo = _split(d["rhs"])
                uw_ = _mm3(p_hi, p_lo, r_hi, r_lo)
                u_sc[d["hh"], slot] = uw_[:, :LANE]
                wq_sc[d["hh"], slot] = jnp.concatenate([uw_[:, LANE:], d["qe"]], axis=0)

        return [vpu, kq] + [double] * n_st + [uw]

    og = og_ref[0]

    def scan_stages(ci, slot):
        st = {}

        def first():
            st["s"] = [s_sc[hh] for hh in range(hb)]
            st["ws_qs"] = [_mm(wq_sc[hh, slot], st["s"][hh]) for hh in range(hb)]

        def state():
            st["v_new"] = [u_sc[hh, slot] - st["ws_qs"][hh][:c] for hh in range(hb)]
            for hh in range(hb):
                s_sc[hh] = ge_sc[hh, slot][0:1, :] * st["s"][hh] + _mm_tn(kd_sc[hh, slot], st["v_new"][hh])

        def out():
            r0 = pl.multiple_of(ci * c, c)
            for hh in range(hb):
                sl = slice(hh * LANE, (hh + 1) * LANE)
                o = st["ws_qs"][hh][c:] + _mm(qk_sc[hh, slot], st["v_new"][hh])
                o = o * lax.rsqrt(jnp.mean(o * o, axis=-1, keepdims=True) + EPS) * og
                o_ref[pl.ds(r0, c), sl] = (o * _silu(z_ref[pl.ds(r0, c), sl])).astype(BF16)

        return [first, state, out]

    def run(stages):
        for f in stages:
            f()

    def step(k, slot_scan):
        prep = prep_stages(k, 1 - slot_scan)
        scan = scan_stages(k - 1, slot_scan)
        run([prep[0], scan[0], prep[1], prep[2], scan[1], prep[3], prep[4], scan[2]] + prep[5:])

    run(prep_stages(0, 0))
    n_mid = nc - 1

    def pair(it, carry):
        step(1 + 2 * it, 0)
        step(2 + 2 * it, 1)
        return carry

    lax.fori_loop(0, n_mid // 2, pair, 0)
    if n_mid % 2:
        step(nc - 1, 0)
    run(scan_stages(nc - 1, (nc - 1) % 2))
    sfin_ref[0] = s_sc[...]


def _delta(proj, bsz, t_len, hist8, s0, conv_w, a_log, dt_bias, onorm_g, layer, *, tb, hb):
    sl = layer if hist8.shape[0] > 1 else 0
    c = CHUNK if t_len % CHUNK == 0 else t_len
    nb = t_len // tb
    hw = hb * LANE
    rows = bsz * t_len

    def col(off):
        return lambda b, g, t: (b * nb + t, off // hw + g)

    def hcol(off):
        return lambda b, g, t: (sl, b, 0, off // hw + g)

    def wcol(off):
        return lambda b, g, t: (layer, 0, off // hw + g)

    smem = pl.BlockSpec(memory_space=pltpu.SMEM)
    kern = functools.partial(_delta_kernel, c=c, tb=tb, hb=hb, layer=layer)
    return pl.pallas_call(
        kern,
        grid=(bsz, H_A // hb, nb),
        in_specs=[
            smem, smem,
            pl.BlockSpec((tb, hw), col(COL_QKV)),
            pl.BlockSpec((tb, hw), col(COL_QKV + 1024)),
            pl.BlockSpec((tb, hw), col(COL_QKV + 2048)),
            pl.BlockSpec((tb, hw), col(COL_ZA)),
            pl.BlockSpec((tb, LANE), lambda b, g, t: (b * nb + t, COL_SMALL // LANE)),
            pl.BlockSpec((1, 1, 8, hw), hcol(0)),
            pl.BlockSpec((1, 1, 8, hw), hcol(1024)),
            pl.BlockSpec((1, 1, 8, hw), hcol(2048)),
            pl.BlockSpec((1, SHORT_CONV, hw), wcol(0)),
            pl.BlockSpec((1, SHORT_CONV, hw), wcol(1024)),
            pl.BlockSpec((1, SHORT_CONV, hw), wcol(2048)),
            pl.BlockSpec((1, 1, LANE), lambda b, g, t: (layer, 0, 0)),
            pl.BlockSpec((1, 1, hb, DK_A, DK_A), lambda b, g, t: (sl, b, g, 0, 0)),
        ],
        out_specs=[
            pl.BlockSpec((tb, hw), lambda b, g, t: (b * nb + t, g)),
            pl.BlockSpec((1, hb, DK_A, DK_A), lambda b, g, t: (b, g, 0, 0)),
        ],
        out_shape=[
            jax.ShapeDtypeStruct((rows, 1024), BF16),
            jax.ShapeDtypeStruct((bsz, H_A, DK_A, DK_A), F32),
        ],
        scratch_shapes=[
            pltpu.VMEM((hb, DK_A, DK_A), F32),
            pltpu.VMEM((tb + 8, hw), F32),
            pltpu.VMEM((tb + 8, hw), F32),
            pltpu.VMEM((tb + 8, hw), F32),
            pltpu.VMEM((hb, 2, c, LANE), F32),
            pltpu.VMEM((hb, 2, 2 * c, LANE), F32),
            pltpu.VMEM((hb, 2, c, LANE), F32),
            pltpu.VMEM((hb, 2, c, c), F32),
            pltpu.VMEM((hb, 2, 8, LANE), F32),
        ],
        compiler_params=_cparams(("parallel", "parallel", "arbitrary")),
        name="delta",
    )(a_log, dt_bias, proj, proj, proj, proj, proj, hist8, hist8, hist8, conv_w, conv_w, conv_w, onorm_g, s0)


def _convb_kernel(glu_ref, halo_ref, hist_ref, z_ref, w_ref, bias_ref, lng_ref, lnb_ref, o_ref, tail_ref, xp_ref,
                  y_ref, *, tb):
    j = pl.program_id(1)

    @pl.when(j == 0)
    def _():
        xp_ref[0:HALO, :] = hist_ref[0, 0]

    @pl.when(j > 0)
    def _():
        hl = halo_ref[...]
        xp_ref[0:HALO, :] = hl[:, :C_B] * _sigmoid(hl[:, C_B:])

    gl = glu_ref[...]
    xp_ref[HALO:HALO + tb, :] = gl[:, :C_B] * _sigmoid(gl[:, C_B:])
    tail_ref[0] = xp_ref[tb:tb + HALO, :]

    rc_n = min(tb, 64)
    cc_n = LANE
    base = HALO - (CONV_B - 1)
    win_n = rc_n + HALO
    for cc in range(C_B // cc_n):
        cs = slice(cc * cc_n, (cc + 1) * cc_n)
        for rc in range(tb // rc_n):
            win = xp_ref[rc * rc_n:rc * rc_n + win_n, cs]
            acc = jnp.zeros((rc_n, cc_n), F32)
            for ph in range(8):
                wph = win if ph == 0 else pltpu.roll(win, win_n - ph, 0)
                for a in range(win_n // 8):
                    k = 8 * a + ph - base
                    if 0 <= k < CONV_B and 8 * a + rc_n <= win_n:
                        acc = acc + w_ref[0, k:k + 1, cs] * wph[8 * a:8 * a + rc_n, :]
            y_ref[rc * rc_n:(rc + 1) * rc_n, cs] = acc

    y = y_ref[...] + bias_ref[0]
    mu = jnp.mean(y, axis=-1, keepdims=True)
    yc = y - mu
    var = jnp.mean(yc * yc, axis=-1, keepdims=True)
    yn = yc * lax.rsqrt(var + EPS) * lng_ref[0] + lnb_ref[0]
    o_ref[...] = (_silu(yn) * _silu(z_ref[...])).astype(BF16)


def _convb(proj, bsz, t_len, hist32, w32, bias, ln_g, ln_b, layer, *, tb):
    sl = layer if hist32.shape[0] > 1 else 0
    nb = t_len // tb
    rows = bsz * t_len
    rb = tb // HALO
    vec = pl.BlockSpec((1, 1, C_B), lambda b, j: (layer, 0, 0))
    kern = functools.partial(_convb_kernel, tb=tb)
    return pl.pallas_call(
        kern,
        grid=(bsz, nb),
        in_specs=[
            pl.BlockSpec((tb, 2 * C_B), lambda b, j: (b * nb + j, COL_GLU // (2 * C_B))),
            pl.BlockSpec((HALO, 2 * C_B), lambda b, j: (jnp.maximum((b * nb + j) * rb - 1, 0), COL_GLU // (2 * C_B))),
            pl.BlockSpec((1, 1, HALO, C_B), lambda b, j: (sl, b, 0, 0)),
            pl.BlockSpec((tb, C_B), lambda b, j: (b * nb + j, COL_ZB // C_B)),
            pl.BlockSpec((1, HALO, C_B), lambda b, j: (layer, 0, 0)),
            vec, vec, vec,
        ],
        out_specs=[
            pl.BlockSpec((tb, C_B), lambda b, j: (b * nb + j, 0)),
            pl.BlockSpec((1, HALO, C_B), lambda b, j: (b, 0, 0)),
        ],
        out_shape=[
            jax.ShapeDtypeStruct((rows, C_B), BF16),
            jax.ShapeDtypeStruct((bsz, HALO, C_B), F32),
        ],
        scratch_shapes=[pltpu.VMEM((HALO + tb, C_B), F32), pltpu.VMEM((tb, C_B), F32)],
        compiler_params=_cparams(("parallel", "arbitrary")),
        name="convb",
    )(proj, proj, hist32, proj, w32, bias, ln_g, ln_b)


SUB = 128
VT_ONES = 16
QK_AHEAD = 3
QK_AHEAD1 = QK_AHEAD + 1


def _tree(parts, op):
    parts = list(parts)
    while len(parts) > 1:
        nxt = [op(parts[n], parts[n + 1]) for n in range(0, len(parts) - 1, 2)]
        if len(parts) % 2:
            nxt.append(parts[-1])
        parts = nxt
    return parts[0]


def _fold8(x, op):
    return _tree([x[r:r + 8, :] for r in range(0, x.shape[0], 8)], op)


def _dsa_t_kernel(qi_ref, wt_ref, ki_ref, q_ref, k_ref, vt_ref, z_ref, o_ref, key_ref, khi_ref, klo_ref, bias_ref,
                  s_ref, p_ref, m_ref, acc_ref, *, tq, n_real, kb, n_keys, n_valid, pos0, topk):
    i = pl.program_id(1)
    nsub = kb // SUB
    last_pos = pos0 + (i + 1) * tq - 1
    lim_last = jnp.minimum((last_pos // CHUNK + 1) * CHUNK, n_valid)
    nkb = jnp.minimum((lim_last + kb - 1) // kb, n_keys // kb)
    qpos = pos0 + i * tq + lax.broadcasted_iota(I32, (1, tq), 1)
    lim = jnp.minimum(((qpos >> 6) + 1) << 6, n_valid)
    sub_iota = lax.broadcasted_iota(I32, (SUB, 1), 0)
    wt = wt_ref[0] * (H_I ** -0.5)

    def p1(kblk, carry):
        for sb in range(nsub):
            r0 = pl.multiple_of(kblk * kb + sb * SUB, SUB)
            kis = ki_ref[0, pl.ds(r0, SUB), :]
            acc = jnp.zeros((SUB, tq), F32)
            for j in range(H_I):
                lg = jnp.dot(kis, qi_ref[0, j * D_I:(j + 1) * D_I, :], preferred_element_type=F32)
                acc = acc + wt[j:j + 1, :] * jnp.maximum(lg, 0.0)
            bits = lax.bitcast_convert_type(jnp.where((r0 + sub_iota) < lim, acc, NEG_INF), I32)
            key = bits ^ ((bits >> 31) & 0x7FFFFFFF)
            ss = slice(sb * SUB, (sb + 1) * SUB)
            key_ref[kblk, ss, :] = key
            khi_ref[kblk, ss, :] = (key >> 16).astype(I16)
            klo_ref[kblk, ss, :] = ((key & 0xFFFF) - 32768).astype(I16)
        return carry

    def p1_pair(it, carry):
        p1(2 * it, carry)
        return p1(2 * it + 1, carry)

    lax.fori_loop(0, nkb // 2, p1_pair, 0)

    @pl.when(nkb % 2 == 1)
    def _():
        p1(nkb - 1, 0)

    def count(pred):
        def body(kblk, cnt):
            parts = [cnt]
            for sb in range(nsub):
                blk = key_ref[kblk, sb * SUB:(sb + 1) * SUB, :]
                parts.append(_fold8(jnp.where(pred(blk, kblk * kb + sb * SUB), 1.0, 0.0), jnp.add))
            return _tree(parts, jnp.add)

        cnt = lax.fori_loop(0, nkb, body, jnp.zeros((8, tq), F32))
        return jnp.sum(cnt, axis=0, keepdims=True)

    def count_ge(cand):
        return count(lambda blk, _: blk >= cand)

    one16 = jnp.ones((16, tq), I16)
    zero16 = jnp.zeros((16, tq), I16)

    def count16_ge(h_ref, cand_s):
        cb = jnp.broadcast_to(cand_s, (16, tq)).astype(I16)

        def body(kblk, cnt):
            parts = [cnt]
            for sb in range(nsub):
                blk = h_ref[kblk, sb * SUB:(sb + 1) * SUB, :]
                parts.append(_tree([jnp.where(blk[r:r + 16, :] >= cb, one16, zero16) for r in range(0, SUB, 16)],
                                   jnp.add))
            return _tree(parts, jnp.add)

        cnt = lax.fori_loop(0, nkb, body, zero16)
        return jnp.sum(cnt.astype(I32).astype(F32), axis=0, keepdims=True)

    def bisect16(h_ref):
        def step(p, t_u):
            cand_u = t_u | jnp.left_shift(jnp.int32(1), 15 - p)
            return jnp.where(count16_ge(h_ref, cand_u - 32768) >= topk, cand_u, t_u)

        return lax.fori_loop(0, 16, step, jnp.zeros((1, tq), I32))

    th_s = bisect16(khi_ref) - 32768
    th16 = jnp.broadcast_to(th_s, (16, tq)).astype(I16)

    def mask_low(kblk, carry):
        for sb in range(nsub):
            for r in range(0, SUB, 16):
                rs = slice(sb * SUB + r, sb * SUB + r + 16)
                hi = khi_ref[kblk, rs, :]
                other = jnp.where(hi > th16, jnp.int16(32767), jnp.int16(-32768))
                klo_ref[kblk, rs, :] = jnp.where(hi == th16, klo_ref[kblk, rs, :], other)
        return carry

    lax.fori_loop(0, nkb, mask_low, 0)
    thr = th_s * 65536 + bisect16(klo_ref)
    thr = jnp.maximum(thr, NEG_INF_KEY + 1)
    cnt_ge = count_ge(thr)

    real_q = lax.broadcasted_iota(I32, (1, tq), 1) < n_real

    @pl.when(jnp.max(jnp.where(real_q, cnt_ge, 0.0)) > topk)
    def _():
        need = topk - count_ge(thr + 1)
        nbits = max(n_keys - 1, 1).bit_length()

        def bis_idx(p, cm):
            cand = cm | jnp.left_shift(jnp.int32(1), nbits - 1 - p)
            below = count(lambda blk, k0: (blk == thr) & ((k0 + sub_iota) < cand))
            return jnp.where(below < need, cand, cm)

        cm = lax.fori_loop(0, nbits, bis_idx, jnp.zeros((1, tq), I32))

        def rewrite(kblk, carry):
            for sb in range(nsub):
                blk = key_ref[kblk, sb * SUB:(sb + 1) * SUB, :]
                drop = (blk == thr) & ((kblk * kb + sb * SUB + sub_iota) > cm)
                key_ref[kblk, sb * SUB:(sb + 1) * SUB, :] = jnp.where(drop, thr - 1, blk)
            return carry

        lax.fori_loop(0, nkb, rewrite, 0)

    m_ref[...] = jnp.full(m_ref.shape, NEG_INF, F32)
    acc_ref[...] = jnp.zeros(acc_ref.shape, F32)
    group = H_C // H_KV
    va = DH_C + VT_ONES

    def p3(kblk, carry):
        k0 = pl.multiple_of(kblk * kb, kb)
        for sb in range(nsub):
            ss = slice(sb * SUB, (sb + 1) * SUB)
            bias_ref[ss, :] = jnp.where(key_ref[kblk, ss, :] >= thr, 0.0, NEG_INF)

        def logits(h):
            g = h // group
            mx = []
            for sb in range(nsub):
                ss = slice(sb * SUB, (sb + 1) * SUB)
                ks = k_ref[0, pl.ds(k0 + sb * SUB, SUB), g * DH_C:(g + 1) * DH_C]
                s = jnp.dot(ks, q_ref[0, h * DH_C:(h + 1) * DH_C, :], preferred_element_type=F32)
                s = s + bias_ref[ss, :]
                s_ref[h % QK_AHEAD1, ss, :] = s
                mx.append(_fold8(s, jnp.maximum))
            return jnp.max(_tree(mx, jnp.maximum), axis=0, keepdims=True)

        m_q = [logits(h) for h in range(QK_AHEAD)]
        for h in range(H_C):
            if h + QK_AHEAD < H_C:
                m_q.append(logits(h + QK_AHEAD))
            m_cur = m_q[h]
            g = h // group
            m_prev = m_ref[h][0:1, :]
            m_new = jnp.maximum(m_prev, m_cur)
            m_safe = jnp.where(m_new == NEG_INF, 0.0, m_new)
            alpha = jnp.exp2((m_prev - m_safe) * Q_SCALE)
            for sb in range(nsub):
                ss = slice(sb * SUB, (sb + 1) * SUB)
                p_ref[ss, :] = jnp.exp2((s_ref[h % QK_AHEAD1, ss, :] - m_safe) * Q_SCALE).astype(BF16)
            pv = jnp.dot(vt_ref[0, kblk, g * va:(g + 1) * va, :], p_ref[...], preferred_element_type=F32)
            acc_ref[h] = alpha * acc_ref[h] + pv
            m_ref[h] = jnp.broadcast_to(m_new, (8, tq))
        return carry

    def p3_pair(it, carry):
        p3(2 * it, carry)
        return p3(2 * it + 1, carry)

    lax.fori_loop(0, nkb // 2, p3_pair, 0)

    @pl.when(nkb % 2 == 1)
    def _():
        p3(nkb - 1, 0)

    for h in range(H_C):
        hs = slice(h * DH_C, (h + 1) * DH_C)
        o = (acc_ref[h, 0:DH_C, :] / acc_ref[h, DH_C:DH_C + 1, :]).T
        o_ref[:, hs] = (o * _silu(z_ref[:, hs])).astype(BF16)


def _dsa_t(z_arr, z_col, qi_t, q_t, wi_t, ki_all, k_all, vt, bsz, t_len, n_valid, pos0, *, tq, n_real, kb):
    n_keys = k_all.shape[1]
    nq = t_len // tq
    rows = bsz * t_len
    topk = min(TOPK, n_valid // 4)
    kern = functools.partial(_dsa_t_kernel, tq=tq, n_real=n_real, kb=kb, n_keys=n_keys, n_valid=n_valid, pos0=pos0,
                             topk=topk)
    return pl.pallas_call(
        kern,
        grid=(bsz, nq),
        in_specs=[
            pl.BlockSpec((1, 512, tq), lambda b, i: (b * nq + i, 0, 0)),
            pl.BlockSpec((1, H_I, tq), lambda b, i: (b * nq + i, 0, 0)),
            pl.BlockSpec((1, n_keys, D_I), lambda b, i: (b, 0, 0)),
            pl.BlockSpec((1, 1024, tq), lambda b, i: (b * nq + i, 0, 0)),
            pl.BlockSpec((1, n_keys, 256), lambda b, i: (b, 0, 0)),
            pl.BlockSpec((1, n_keys // kb, H_KV * (DH_C + VT_ONES), kb), lambda b, i: (b, 0, 0, 0)),
            pl.BlockSpec((tq, 1024), lambda b, i: (b * nq + i, z_col)),
        ],
        out_specs=pl.BlockSpec((tq, 1024), lambda b, i: (b * nq + i, 0)),
        out_shape=jax.ShapeDtypeStruct((rows, 1024), BF16),
        scratch_shapes=[
            pltpu.VMEM((n_keys // kb, kb, tq), I32),
            pltpu.VMEM((n_keys // kb, kb, tq), I16),
            pltpu.VMEM((n_keys // kb, kb, tq), I16),
            pltpu.VMEM((kb, tq), F32),
            pltpu.VMEM((QK_AHEAD1, kb, tq), F32),
            pltpu.VMEM((kb, tq), BF16),
            pltpu.VMEM((H_C, 8, tq), F32),
            pltpu.VMEM((H_C, DH_C + VT_ONES, tq), F32),
        ],
        compiler_params=_cparams(("parallel", "arbitrary")),
        name="dsa_t",
    )(qi_t, wi_t, ki_all, q_t, k_all, vt, z_arr)


def _merge_kernel(oa_ref, ub_ref, oc_ref, gate_ref, x_ref, woa_ref, wpw_ref, woc_ref, wout_ref, fg_ref, xo_ref,
                  *rest, final):
    ya = jnp.dot(oa_ref[...], woa_ref[0], preferred_element_type=F32)
    mix = _sigmoid(gate_ref[:, 0:D_MODEL]) * ya
    yb = jnp.dot(ub_ref[...], wpw_ref[0], preferred_element_type=F32)
    mix = mix + _sigmoid(gate_ref[:, D_MODEL:2 * D_MODEL]) * yb
    yc = jnp.dot(oc_ref[...], woc_ref[0], preferred_element_type=F32)
    mix = mix + _sigmoid(gate_ref[:, 2 * D_MODEL:3 * D_MODEL]) * yc
    xn = x_ref[...] + jnp.dot(mix.astype(BF16), wout_ref[0], preferred_element_type=F32)
    xo_ref[...] = xn
    if final:
        ms = jnp.mean(xn * xn, axis=-1, keepdims=True)
        rest[0][...] = xn * lax.rsqrt(ms + EPS) * fg_ref[...]


def _merge(o_a, u_b, o_c, proj, x2d, w_oa, w_pw, w_oc, w_out, final_g, layer, final):
    rows = x2d.shape[0]
    tm = _pick(rows, (512, 256, 128, 64, 32, 16, 8))
    act = pl.BlockSpec((tm, D_MODEL), lambda i: (i, 0))
    wsp = pl.BlockSpec((1, D_MODEL, D_MODEL), lambda i: (layer, 0, 0))
    out_specs = [act]
    out_shape = [jax.ShapeDtypeStruct((rows, D_MODEL), F32)]
    if final:
        out_specs.append(act)
        out_shape.append(jax.ShapeDtypeStruct((rows, D_MODEL), F32))
    res = pl.pallas_call(
        functools.partial(_merge_kernel, final=final),
        grid=(rows // tm,),
        in_specs=[act, act, act, pl.BlockSpec((tm, 3 * D_MODEL), lambda i: (i, COL_GATE // (3 * D_MODEL))), act,
                  wsp, wsp, wsp, wsp, pl.BlockSpec((1, D_MODEL), lambda i: (0, 0))],
        out_specs=out_specs,
        out_shape=out_shape,
        compiler_params=_cparams(("parallel",)),
        name="merge",
    )(o_a, u_b, o_c, proj, x2d, w_oa, w_pw, w_oc, w_out, final_g)
    return res if final else (res[0], None)


def _permute_w_in(w_in):
    offs = [0]
    for s in IN_SIZES:
        offs.append(offs[-1] + s)
    (qkv, a_in, b_in, z_a, glu, z_b, q_c, k_c, v_c, qi, ki, wi, z_c, gate) = [
        w_in[..., offs[n]:offs[n + 1]] for n in range(len(IN_SIZES))]

    def zeros(n):
        return jnp.zeros(w_in.shape[:-1] + (n,), BF16)

    parts = [qkv, gate, glu, z_a, z_b, q_c, z_c, qi, k_c, v_c, a_in, b_in, wi, zeros(SM_KI - SM_WI - H_I), ki]
    used = sum(p.shape[-1] for p in parts)
    return jnp.concatenate([p.astype(BF16) for p in parts] + [zeros(NP - used)], axis=-1)


def _stream_layer(x2d, layer, final, cfg, state, prm, stacks):
    bsz, t_len, pos0 = cfg["bsz"], cfg["t_len"], cfg["pos0"]
    proj = _inproj(x2d, prm["norm_g"], prm["w_in"], layer)

    o_a, delta_new = _delta(proj, bsz, t_len, state["conv_a8"], state["delta"], prm["conv_a_w"], prm["a_log"],
                            prm["dt_bias"], prm["onorm_a_g"], layer, tb=cfg["delta_tb"], hb=cfg["delta_hb"])
    conv_a_new = proj.reshape(bsz, t_len, NP)[:, t_len - (SHORT_CONV - 1):, COL_QKV:COL_QKV + 3 * D_MODEL]

    u_b, tail_b = _convb(proj, bsz, t_len, state["conv_b32"], prm["conv_b_w32"], prm["conv_b_bias"], prm["ln_b_g"],
                         prm["ln_b_b"], layer, tb=cfg["convb_tb"])
    conv_b_new = tail_b[:, HALO - (CONV_B - 1):, :]

    n_valid = pos0 + t_len
    kb = cfg["dsa_kb"]
    tq = cfg["dsa_tq"]
    n_keys = -(-n_valid // kb) * kb
    direct = tq == cfg["dsa_tq_real"] and pos0 == 0 and n_keys == n_valid
    q_r, qi_r, kst, vst, kist, k_bf, ki_bf, v_x = _rope(
        proj, stacks, layer, bsz, t_len, pos0, tb=kb if direct else _pick(t_len, (512, 256, 128, 64, 32)),
        tq_t=tq if direct else None)
    wi = proj[:, COL_SMALL + SM_WI:COL_SMALL + SM_WI + H_I]
    if direct:
        wi_t = wi.reshape(bsz * t_len // tq, tq, H_I).transpose(0, 2, 1)
        o_c = _dsa_t(proj, COL_ZC // 1024, qi_r, q_r, wi_t, ki_bf, k_bf, v_x, bsz, t_len, n_valid, pos0, tq=tq,
                     n_real=tq, kb=kb)
    else:
        def with_past(past, new, width):
            parts = [past, new] if pos0 else [new]
            if n_keys > n_valid:
                parts.append(jnp.zeros((bsz, n_keys - n_valid, width), BF16))
            return jnp.concatenate(parts, axis=1) if len(parts) > 1 else new

        def pad_q(a):
            a = a.reshape(bsz, t_len, a.shape[-1])
            return jnp.pad(a, ((0, 0), (0, tq - t_len), (0, 0))).reshape(bsz * tq, a.shape[-1])

        def pad_qt(a):
            return pad_q(a).reshape(bsz, tq, a.shape[-1]).transpose(0, 2, 1)

        k_all = with_past(state.get("past_k"), k_bf, 256)
        ki_all = with_past(state.get("past_ki"), ki_bf, D_I)
        vt = with_past(state.get("past_v"), v_x, 256).reshape(bsz, n_keys // kb, kb, H_KV, DH_C).transpose(0, 1, 3, 4, 2)
        vt = jnp.concatenate([vt, jnp.ones((bsz, n_keys // kb, H_KV, VT_ONES, kb), BF16)], axis=3)
        vt = vt.reshape(bsz, n_keys // kb, H_KV * (DH_C + VT_ONES), kb)
        o_c = _dsa_t(pad_q(proj[:, COL_ZC:COL_ZC + 1024]), 0, pad_qt(qi_r), pad_qt(q_r), pad_qt(wi), ki_all, k_all, vt, bsz, tq,
                     n_valid, pos0, tq=tq, n_real=t_len, kb=kb)
        o_c = o_c.reshape(bsz, tq, 1024)[:, :t_len].reshape(bsz * t_len, 1024)

    x_new, y_fin = _merge(o_a, u_b, o_c, proj, x2d, prm["w_o_a"], prm["w_pw2_b"], prm["w_o_c"], prm["w_out"],
                          prm["final_norm_g"], layer, final)
    return x_new, y_fin, (conv_a_new, delta_new, conv_b_new), (kst, vst, kist)


def _run(x_prompt, x_sample, cache_k, cache_v, cache_idx_k, state_conv_a, state_delta, state_conv_b, prm, cfg_p,
         cfg_s):
    n_layers = prm["norm_g"].shape[0]
    bp, tp = cfg_p["bsz"], cfg_p["t_len"]
    bs, ts = cfg_s["bsz"], cfg_s["t_len"]
    past = cfg_s["pos0"]
    xp = x_prompt.reshape(bp * tp, D_MODEL)
    xs = x_sample.reshape(bs * ts, D_MODEL)
    zero_state = {
        "conv_a8": jnp.zeros((1, bp, 8, 3 * D_MODEL), F32),
        "delta": jnp.zeros((1, bp, H_A, DK_A, DK_A), F32),
        "conv_b32": jnp.zeros((1, bp, HALO, C_B), F32),
    }
    conv_a8_s = jnp.pad(state_conv_a, ((0, 0), (0, 0), (8 - (SHORT_CONV - 1), 0), (0, 0)))
    conv_b32_s = jnp.pad(state_conv_b, ((0, 0), (0, 0), (HALO - (CONV_B - 1), 0), (0, 0)))
    past_k = cache_k.reshape(n_layers, bs, past, 256).astype(BF16)
    past_v = cache_v.reshape(n_layers, bs, past, 256).astype(BF16)
    past_ki = cache_idx_k.astype(BF16)

    def empty_stacks(bsz, t_len):
        return (jnp.zeros((n_layers, bsz, t_len, 256), F32), jnp.zeros((n_layers, bsz, t_len, 256), F32),
                jnp.zeros((n_layers, bsz, t_len, D_I), F32))

    stk_p, stk_s = empty_stacks(bp, tp), empty_stacks(bs, ts)
    st_p, st_s = [], []
    yp = ys = None
    for layer in range(n_layers):
        final = layer == n_layers - 1
        xp, yp, sp, stk_p = _stream_layer(xp, layer, final, cfg_p, zero_state, prm, stk_p)
        s_state = {"conv_a8": conv_a8_s, "delta": state_delta, "conv_b32": conv_b32_s,
                   "past_k": past_k[layer], "past_v": past_v[layer], "past_ki": past_ki[layer]}
        xs, ys, ss, stk_s = _stream_layer(xs, layer, final, cfg_s, s_state, prm, stk_s)
        st_p.append(sp)
        st_s.append(ss)

    def outputs(stk, states, bsz, t_len):
        kst, vst, kist = stk
        return (kst.reshape(n_layers, bsz, t_len, H_KV, DH_C), vst.reshape(n_layers, bsz, t_len, H_KV, DH_C), kist
                ) + tuple(jnp.stack([st[n] for st in states], axis=0) for n in range(3))

    return ((yp.reshape(bp, tp, D_MODEL), ys.reshape(bs, ts, D_MODEL))
            + outputs(stk_p, st_p, bp, tp) + outputs(stk_s, st_s, bs, ts))


def _prep_params(norm_g, w_in, conv_a_w, a_log, dt_bias, onorm_a_g, w_o_a, conv_b_w, conv_b_bias, ln_b_g, ln_b_b,
                 w_pw2_b, w_o_c, w_out, final_norm_g):
    return {
        "norm_g": norm_g[:, None, :], "w_in": _permute_w_in(w_in), "conv_a_w": conv_a_w, "a_log": a_log, "dt_bias": dt_bias,
        "onorm_a_g": onorm_a_g[:, None, :], "w_o_a": w_o_a.astype(BF16),
        "conv_b_w32": jnp.pad(conv_b_w, ((0, 0), (0, HALO - CONV_B), (0, 0))),
        "conv_b_bias": conv_b_bias[:, None, :], "ln_b_g": ln_b_g[:, None, :], "ln_b_b": ln_b_b[:, None, :],
        "w_pw2_b": w_pw2_b.astype(BF16),
        "w_o_c": w_o_c.astype(BF16), "w_out": w_out.astype(BF16), "final_norm_g": final_norm_g[None, :],
    }


def _stream_cfg(bsz, t_len, pos0):
    c = CHUNK if t_len % CHUNK == 0 else t_len
    n_valid = pos0 + t_len
    if t_len % 256 == 0:
        tq, kb = 256, 512
    else:
        tq = -(-t_len // LANE) * LANE
        kb = 384 if n_valid > 384 else LANE
    delta_tb = _pick(t_len, (512, 256, 128, 64, t_len))
    if delta_tb % c:
        delta_tb = t_len
    return {
        "bsz": bsz, "t_len": t_len, "pos0": pos0,
        "delta_tb": delta_tb, "delta_hb": H_A,
        "convb_tb": _pick(t_len, (256, 128, 64, 32)),
        "dsa_tq": tq, "dsa_tq_real": min(tq, t_len), "dsa_kb": kb,
    }


def kernel(x_prompt, x_sample, cache_k, cache_v, cache_idx_k, state_conv_a, state_delta, state_conv_b, norm_g, w_in,
           conv_a_w, a_log, dt_bias, onorm_a_g, w_o_a, conv_b_w, conv_b_bias, ln_b_g, ln_b_b, w_pw2_b, w_o_c, w_out,
           final_norm_g):
    prm = _prep_params(norm_g, w_in, conv_a_w, a_log, dt_bias, onorm_a_g, w_o_a, conv_b_w, conv_b_bias, ln_b_g,
                       ln_b_b, w_pw2_b, w_o_c, w_out, final_norm_g)
    cfg_p = _stream_cfg(x_prompt.shape[0], x_prompt.shape[1], 0)
    cfg_s = _stream_cfg(x_sample.shape[0], x_sample.shape[1], cache_k.shape[2])
    return _run(x_prompt, x_sample, cache_k, cache_v, cache_idx_k, state_conv_a, state_delta, state_conv_b, prm,
                cfg_p, cfg_s)
```

```python
import functools
import math

import jax
import jax.numpy as jnp
from jax import lax
from jax.experimental import pallas as pl
from jax.experimental.pallas import tpu as pltpu

F32 = jnp.float32
BF16 = jnp.bfloat16
I32 = jnp.int32
I16 = jnp.int16

D_MODEL = 1024
CHUNK = 64
H_A = 8
DK_A = 128
SHORT_CONV = 4
C_B = 1024
CONV_B = 31
H_C = 8
DH_C = 128
H_KV = 2
H_I = 8
D_I = 64
TOPK = 256
ROPE_THETA = 500000.0
ROPE_FRACTION = 4
EPS = 1e-6
IN_SIZES = (3 * D_MODEL, H_A, H_A, D_MODEL, 2 * C_B, C_B, H_C * DH_C, H_KV * DH_C, H_KV * DH_C, H_I * D_I, D_I, H_I,
            H_C * DH_C, 3 * D_MODEL)

LANE = 128
HALO = 32

COL_QKV = 0
COL_GATE = 3072
COL_GLU = 6144
COL_ZA = 8192
COL_ZB = 9216
COL_QC = 10240
COL_ZC = 11264
COL_QI = 12288
COL_KC = 12800
COL_VC = 13056
COL_SMALL = 13312
NP = 13824
SM_A, SM_B, SM_WI, SM_KI = 0, 8, 16, 64

NEG_INF = float("-inf")
NEG_INF_KEY = -2139095041

VMEM_LIMIT = 56 * 1024 * 1024
Q_SCALE = (DH_C ** -0.5) * math.log2(math.e)


def _cparams(sem):
    return pltpu.CompilerParams(dimension_semantics=sem, vmem_limit_bytes=VMEM_LIMIT)


def _pick(n, cands):
    for c in cands:
        if n % c == 0:
            return c
    raise ValueError(f"no block size in {cands} divides {n}")


def _sigmoid(x):
    return jax.nn.sigmoid(x)


def _silu(x):
    return x * jax.nn.sigmoid(x)


def _inproj_kernel(x_ref, g_ref, w_ref, o_ref, xn_ref):
    @pl.when(pl.program_id(1) == 0)
    def _():
        x = x_ref[...]
        ms = jnp.mean(x * x, axis=-1, keepdims=True)
        xn_ref[...] = (x * lax.rsqrt(ms + EPS) * g_ref[0]).astype(BF16)

    o_ref[...] = jnp.dot(xn_ref[...], w_ref[0], preferred_element_type=F32)


def _inproj(x2d, g_all, w_all, layer):
    rows = x2d.shape[0]
    tm = _pick(rows, (2048, 1024, 512, 256, 128, 64, 32, 16, 8))
    tn = 768
    return pl.pallas_call(
        _inproj_kernel,
        grid=(rows // tm, NP // tn),
        in_specs=[
            pl.BlockSpec((tm, D_MODEL), lambda i, j: (i, 0)),
            pl.BlockSpec((1, 1, D_MODEL), lambda i, j: (layer, 0, 0)),
            pl.BlockSpec((1, D_MODEL, tn), lambda i, j: (layer, 0, j)),
        ],
        out_specs=pl.BlockSpec((tm, tn), lambda i, j: (i, j)),
        out_shape=jax.ShapeDtypeStruct((rows, NP), F32),
        scratch_shapes=[pltpu.VMEM((tm, D_MODEL), BF16)],
        compiler_params=_cparams(("parallel", "arbitrary")),
        name="inproj",
    )(x2d, g_all, w_all)


def _rope_kernel(q_ref, k_ref, v_ref, qi_ref, sm_ref, cq_ref, sq_ref, ci_ref, si_ref, kst_in, vst_in, kist_in,
                 qo_ref, qio_ref, kst_ref, vst_ref, kist_ref, kbf_ref, kibf_ref, vx_ref, *, tq_t):
    del kst_in, vst_in, kist_in
    tb = q_ref.shape[0]
    lane = lax.broadcasted_iota(I32, (tb, LANE), 1)
    l64 = lane & 63
    cq, sq, ci, si = cq_ref[...], sq_ref[...], ci_ref[...], si_ref[...]

    def rot_qk(x):
        sw = jnp.where(lane < 16, pltpu.roll(x, LANE - 16, 1), pltpu.roll(x, 16, 1))
        return jnp.where(lane < 32, x * cq + sw * sq, x)

    def rot_idx(x):
        sw = jnp.where(l64 < 8, pltpu.roll(x, LANE - 8, 1), pltpu.roll(x, 8, 1))
        return jnp.where(l64 < 16, x * ci + sw * si, x)

    def put_q(o_ref, h, x):
        if tq_t:
            for n in range(tb // tq_t):
                o_ref[n, h * LANE:(h + 1) * LANE, :] = x[n * tq_t:(n + 1) * tq_t, :].T.astype(BF16)
        else:
            o_ref[:, h * LANE:(h + 1) * LANE] = x.astype(BF16)

    for h in range(H_C):
        put_q(qo_ref, h, rot_qk(q_ref[:, h * LANE:(h + 1) * LANE]))
    va = DH_C + VT_ONES
    for h in range(H_KV):
        sl = slice(h * LANE, (h + 1) * LANE)
        kr = rot_qk(k_ref[:, sl])
        kst_ref[0, 0, :, sl] = kr
        kbf_ref[0, :, sl] = kr.astype(BF16)
        v = v_ref[:, sl]
        vst_ref[0, 0, :, sl] = v
        if tq_t:
            vx_ref[0, 0, h * va:h * va + DH_C, :] = v.T.astype(BF16)
            vx_ref[0, 0, h * va + DH_C:(h + 1) * va, :] = jnp.ones((VT_ONES, tb), BF16)
        else:
            vx_ref[0, :, sl] = v.astype(BF16)
    for h in range(H_I * D_I // LANE):
        put_q(qio_ref, h, rot_idx(qi_ref[:, h * LANE:(h + 1) * LANE]) * (D_I ** -0.5))
    ki = rot_idx(sm_ref[...])[:, SM_KI:SM_KI + D_I]
    kist_ref[0, 0] = ki
    kibf_ref[0] = ki.astype(BF16)


def _rope_tables(t_len, pos0):
    pos = (pos0 + jnp.arange(t_len)).astype(F32)

    def table(d):
        rot = d // ROPE_FRACTION
        half = rot // 2
        inv = ROPE_THETA ** (-jnp.arange(half, dtype=F32) * 2.0 / rot)
        ang = pos[:, None] * inv[None, :]
        cos, sin = jnp.cos(ang), jnp.sin(ang)
        c = jnp.concatenate([cos, cos, jnp.ones((t_len, d - rot), F32)], axis=1)
        s = jnp.concatenate([-sin, sin, jnp.zeros((t_len, d - rot), F32)], axis=1)
        return jnp.tile(c, (1, LANE // d)), jnp.tile(s, (1, LANE // d))

    cq, sq = table(DH_C)
    ci, si = table(D_I)
    return cq, sq, ci, si


def _rope(proj, stacks, layer, bsz, t_len, pos0, *, tb, tq_t):
    nb = t_len // tb
    cq, sq, ci, si = _rope_tables(t_len, pos0)
    rows = bsz * t_len
    kst, vst, kist = stacks
    tab = pl.BlockSpec((tb, LANE), lambda b, j: (j, 0))
    anyspec = pl.BlockSpec(memory_space=pl.ANY)
    va2 = H_KV * (DH_C + VT_ONES)
    if tq_t:
        nt = tb // tq_t
        vx_spec = pl.BlockSpec((1, 1, va2, tb), lambda b, j: (b, j, 0, 0))
        vx_shape = jax.ShapeDtypeStruct((bsz, nb, va2, tb), BF16)
        q_spec = pl.BlockSpec((nt, 1024, tq_t), lambda b, j: (b * nb + j, 0, 0))
        q_shape = jax.ShapeDtypeStruct((rows // tq_t, 1024, tq_t), BF16)
        qi_spec = pl.BlockSpec((nt, 512, tq_t), lambda b, j: (b * nb + j, 0, 0))
        qi_shape = jax.ShapeDtypeStruct((rows // tq_t, 512, tq_t), BF16)
    else:
        vx_spec = pl.BlockSpec((1, tb, 256), lambda b, j: (b, j, 0))
        vx_shape = jax.ShapeDtypeStruct((bsz, t_len, 256), BF16)
        q_spec = pl.BlockSpec((tb, 1024), lambda b, j: (b * nb + j, 0))
        q_shape = jax.ShapeDtypeStruct((rows, 1024), BF16)
        qi_spec = pl.BlockSpec((tb, 512), lambda b, j: (b * nb + j, 0))
        qi_shape = jax.ShapeDtypeStruct((rows, 512), BF16)
    return pl.pallas_call(
        functools.partial(_rope_kernel, tq_t=tq_t),
        grid=(bsz, nb),
        in_specs=[
            pl.BlockSpec((tb, 1024), lambda b, j: (b * nb + j, COL_QC // 1024)),
            pl.BlockSpec((tb, 256), lambda b, j: (b * nb + j, COL_KC // 256)),
            pl.BlockSpec((tb, 256), lambda b, j: (b * nb + j, COL_VC // 256)),
            pl.BlockSpec((tb, 512), lambda b, j: (b * nb + j, COL_QI // 512)),
            pl.BlockSpec((tb, LANE), lambda b, j: (b * nb + j, COL_SMALL // LANE)),
            tab, tab, tab, tab,
            anyspec, anyspec, anyspec,
        ],
        out_specs=[
            q_spec,
            qi_spec,
            pl.BlockSpec((1, 1, tb, 256), lambda b, j: (layer, b, j, 0)),
            pl.BlockSpec((1, 1, tb, 256), lambda b, j: (layer, b, j, 0)),
            pl.BlockSpec((1, 1, tb, D_I), lambda b, j: (layer, b, j, 0)),
            pl.BlockSpec((1, tb, 256), lambda b, j: (b, j, 0)),
            pl.BlockSpec((1, tb, D_I), lambda b, j: (b, j, 0)),
            vx_spec,
        ],
        out_shape=[
            q_shape,
            qi_shape,
            jax.ShapeDtypeStruct(kst.shape, F32),
            jax.ShapeDtypeStruct(vst.shape, F32),
            jax.ShapeDtypeStruct(kist.shape, F32),
            jax.ShapeDtypeStruct((bsz, t_len, 256), BF16),
            jax.ShapeDtypeStruct((bsz, t_len, D_I), BF16),
            vx_shape,
        ],
        input_output_aliases={9: 2, 10: 3, 11: 4},
        compiler_params=_cparams(("parallel", "parallel")),
        name="rope",
    )(proj, proj, proj, proj, proj, cq, sq, ci, si, kst, vst, kist)


def _mm(a, b):
    return jnp.dot(a.astype(BF16), b.astype(BF16), preferred_element_type=F32)


def _mm_nt(a, b):
    return lax.dot_general(a.astype(BF16), b.astype(BF16), (((1,), (1,)), ((), ())), preferred_element_type=F32)


def _mm_tn(a, b):
    return lax.dot_general(a.astype(BF16), b.astype(BF16), (((0,), (0,)), ((), ())), preferred_element_type=F32)


def _split(x):
    hi = x.astype(BF16)
    return hi, (x - hi.astype(F32)).astype(BF16)


def _mm3(a_hi, a_lo, b_hi, b_lo):
    def d(x, y):
        return jnp.dot(x, y, preferred_element_type=F32)
    return d(a_hi, b_hi) + (d(a_hi, b_lo) + d(a_lo, b_hi))


def _delta_kernel(alog_ref, dtb_ref, q_ref, k_ref, v_ref, z_ref, sm_ref, hq_ref, hk_ref, hv_ref, wq_ref, wk_ref,
                  wv_ref, og_ref, s0_ref, o_ref, sfin_ref, s_sc, xq_sc, xk_sc, xv_sc, u_sc, wq_sc, kd_sc,
                  qk_sc, ge_sc, *, c, tb, hb, layer):
    hg = pl.program_id(1)
    tblk = pl.program_id(2)
    nc = tb // c
    n_st = int(math.log2(c))

    @pl.when(tblk == 0)
    def _():
        s_sc[...] = s0_ref[0, 0]
        xq_sc[0:8, :] = hq_ref[0, 0]
        xk_sc[0:8, :] = hk_ref[0, 0]
        xv_sc[0:8, :] = hv_ref[0, 0]

    @pl.when(tblk > 0)
    def _():
        xq_sc[0:8, :] = xq_sc[tb:tb + 8, :]
        xk_sc[0:8, :] = xk_sc[tb:tb + 8, :]
        xv_sc[0:8, :] = xv_sc[tb:tb + 8, :]

    xq_sc[8:8 + tb, :] = q_ref[...]
    xk_sc[8:8 + tb, :] = k_ref[...]
    xv_sc[8:8 + tb, :] = v_ref[...]

    lane = lax.broadcasted_iota(I32, (c, LANE), 1)
    row = lax.broadcasted_iota(I32, (c, LANE), 0)
    ri = lax.broadcasted_iota(I32, (c, c), 0)
    cj = lax.broadcasted_iota(I32, (c, c), 1)
    tri_incl = ri >= cj
    tri_strict = ri > cj
    eye = jnp.where(ri == cj, 1.0, 0.0).astype(F32)

    def cumsum_rows(x):
        s = 1
        while s < c:
            x = x + jnp.where(row >= s, pltpu.roll(x, s, 0), 0.0)
            s *= 2
        return x

    def conv_silu(x_sc, w_ref, r0, sl):
        win = x_sc[pl.ds(r0, c + 8), sl]
        acc = win[8:, :] * w_ref[0, SHORT_CONV - 1:SHORT_CONV, sl]
        for jj in range(SHORT_CONV - 1):
            sh = SHORT_CONV - 1 - jj
            acc = acc + pltpu.roll(win, sh, 0)[8:, :] * w_ref[0, jj:jj + 1, sl]
        return _silu(acc)

    def l2n(x):
        return x * lax.rsqrt(jnp.sum(x * x, axis=-1, keepdims=True) + EPS)

    def gates(ci, hh):
        h = hg * hb + hh
        sm = sm_ref[pl.ds(pl.multiple_of(ci * c, c), c), :]
        a_col = jnp.sum(jnp.where(lane == SM_A + h, sm, 0.0), axis=1, keepdims=True)
        b_col = jnp.sum(jnp.where(lane == SM_B + h, sm, 0.0), axis=1, keepdims=True)
        a_b = jnp.broadcast_to(a_col, (c, LANE)) + dtb_ref[layer, h]
        softplus = jnp.maximum(a_b, 0.0) + jnp.log1p(jnp.exp(-jnp.abs(a_b)))
        g_b = -jnp.exp(jnp.full((c, LANE), alog_ref[layer, h], F32)) * softplus
        return cumsum_rows(g_b), jnp.broadcast_to(_sigmoid(b_col), (c, LANE))

    def prep_stages(ci, slot):
        st = []

        def vpu():
            for hh in range(hb):
                r0 = pl.multiple_of(ci * c, c)
                sl = slice(hh * LANE, (hh + 1) * LANE)
                q = l2n(conv_silu(xq_sc, wq_ref, r0, sl)) * (DK_A ** -0.5)
                k = l2n(conv_silu(xk_sc, wk_ref, r0, sl))
                v = conv_silu(xv_sc, wv_ref, r0, sl)
                big_g, beta_b = gates(ci, hh)
                g_row = jnp.sum(eye * big_g[:, :c], axis=0, keepdims=True)
                edec = jnp.exp(jnp.minimum(big_g[:, :c] - g_row, 0.0))
                kb = k * beta_b
                eg = jnp.exp(big_g)
                g_last = big_g[c - 1:c, :]
                kd_sc[hh, slot] = k * jnp.exp(g_last - big_g)
                ge_sc[hh, slot] = jnp.broadcast_to(jnp.exp(g_last), (8, LANE))
                st.append(dict(hh=hh, q=q, k=k, kb=kb, edec=edec, qe=q * eg,
                               rhs=jnp.concatenate([v * beta_b, kb * eg], axis=1)))

        def kq():
            for d in st:
                kq_ = _mm_nt(jnp.concatenate([d["kb"], d["q"]], axis=0), d["k"])
                qk_sc[d["hh"], slot] = kq_[c:] * jnp.where(tri_incl, d["edec"], 0.0)
                m = -(kq_[:c] * jnp.where(tri_strict, d["edec"], 0.0))
                d["x"] = jnp.concatenate([m, eye], axis=0)

        def double():
            for d in st:
                x_hi, x_lo = _split(d["x"])
                y = _mm3(x_hi, x_lo, x_hi[:c], x_lo[:c])
                d["x"] = jnp.concatenate([y[:c], d["x"][c:] + y[c:]], axis=0)

        def uw():
            for d in st:
                p_hi, p_lo = _split(d["x"][c:])
                r_hi, r_lo = _split(d["rhs"])
                uw_ = _mm3(p_hi, p_lo, r_hi, r_lo)
                u_sc[d["hh"], slot] = uw_[:, :LANE]
                wq_sc[d["hh"], slot] = jnp.concatenate([uw_[:, LANE:], d["qe"]], axis=0)

        return [vpu, kq] + [double] * n_st + [uw]

    og = og_ref[0]

    def scan_stages(ci, slot):
        st = {}

        def first():
            st["s"] = [s_sc[hh] for hh in range(hb)]
            st["ws_qs"] = [_mm(wq_sc[hh, slot], st["s"][hh]) for hh in range(hb)]

        def state():
            st["v_new"] = [u_sc[hh, slot] - st["ws_qs"][hh][:c] for hh in range(hb)]
            for hh in range(hb):
                s_sc[hh] = ge_sc[hh, slot][0:1, :] * st["s"][hh] + _mm_tn(kd_sc[hh, slot], st["v_new"][hh])

        def out():
            r0 = pl.multiple_of(ci * c, c)
            for hh in range(hb):
                sl = slice(hh * LANE, (hh + 1) * LANE)
                o = st["ws_qs"][hh][c:] + _mm(qk_sc[hh, slot], st["v_new"][hh])
                o = o * lax.rsqrt(jnp.mean(o * o, axis=-1, keepdims=True) + EPS) * og
                o_ref[pl.ds(r0, c), sl] = (o * _silu(z_ref[pl.ds(r0, c), sl])).astype(BF16)

        return [first, state, out]

    def run(stages):
        for f in stages:
            f()

    def step(k, slot_scan):
        prep = prep_stages(k, 1 - slot_scan)
        scan = scan_stages(k - 1, slot_scan)
        run([prep[0], scan[0], prep[1], prep[2], scan[1], prep[3], prep[4], scan[2]] + prep[5:])

    run(prep_stages(0, 0))
    n_mid = nc - 1

    def pair(it, carry):
        step(1 + 2 * it, 0)
        step(2 + 2 * it, 1)
        return carry

    lax.fori_loop(0, n_mid // 2, pair, 0)
    if n_mid % 2:
        step(nc - 1, 0)
    run(scan_stages(nc - 1, (nc - 1) % 2))
    sfin_ref[0] = s_sc[...]


def _delta(proj, bsz, t_len, hist8, s0, conv_w, a_log, dt_bias, onorm_g, layer, *, tb, hb):
    sl = layer if hist8.shape[0] > 1 else 0
    c = CHUNK if t_len % CHUNK == 0 else t_len
    nb = t_len // tb
    hw = hb * LANE
    rows = bsz * t_len

    def col(off):
        return lambda b, g, t: (b * nb + t, off // hw + g)

    def hcol(off):
        return lambda b, g, t: (sl, b, 0, off // hw + g)

    def wcol(off):
        return lambda b, g, t: (layer, 0, off // hw + g)

    smem = pl.BlockSpec(memory_space=pltpu.SMEM)
    kern = functools.partial(_delta_kernel, c=c, tb=tb, hb=hb, layer=layer)
    return pl.pallas_call(
        kern,
        grid=(bsz, H_A // hb, nb),
        in_specs=[
            smem, smem,
            pl.BlockSpec((tb, hw), col(COL_QKV)),
            pl.BlockSpec((tb, hw), col(COL_QKV + 1024)),
            pl.BlockSpec((tb, hw), col(COL_QKV + 2048)),
            pl.BlockSpec((tb, hw), col(COL_ZA)),
            pl.BlockSpec((tb, LANE), lambda b, g, t: (b * nb + t, COL_SMALL // LANE)),
            pl.BlockSpec((1, 1, 8, hw), hcol(0)),
            pl.BlockSpec((1, 1, 8, hw), hcol(1024)),
            pl.BlockSpec((1, 1, 8, hw), hcol(2048)),
            pl.BlockSpec((1, SHORT_CONV, hw), wcol(0)),
            pl.BlockSpec((1, SHORT_CONV, hw), wcol(1024)),
            pl.BlockSpec((1, SHORT_CONV, hw), wcol(2048)),
            pl.BlockSpec((1, 1, LANE), lambda b, g, t: (layer, 0, 0)),
            pl.BlockSpec((1, 1, hb, DK_A, DK_A), lambda b, g, t: (sl, b, g, 0, 0)),
        ],
        out_specs=[
            pl.BlockSpec((tb, hw), lambda b, g, t: (b * nb + t, g)),
            pl.BlockSpec((1, hb, DK_A, DK_A), lambda b, g, t: (b, g, 0, 0)),
        ],
        out_shape=[
            jax.ShapeDtypeStruct((rows, 1024), BF16),
            jax.ShapeDtypeStruct((bsz, H_A, DK_A, DK_A), F32),
        ],
        scratch_shapes=[
            pltpu.VMEM((hb, DK_A, DK_A), F32),
            pltpu.VMEM((tb + 8, hw), F32),
            pltpu.VMEM((tb + 8, hw), F32),
            pltpu.VMEM((tb + 8, hw), F32),
            pltpu.VMEM((hb, 2, c, LANE), F32),
            pltpu.VMEM((hb, 2, 2 * c, LANE), F32),
            pltpu.VMEM((hb, 2, c, LANE), F32),
            pltpu.VMEM((hb, 2, c, c), F32),
            pltpu.VMEM((hb, 2, 8, LANE), F32),
        ],
        compiler_params=_cparams(("parallel", "parallel", "arbitrary")),
        name="delta",
    )(a_log, dt_bias, proj, proj, proj, proj, proj, hist8, hist8, hist8, conv_w, conv_w, conv_w, onorm_g, s0)


def _convb_kernel(glu_ref, halo_ref, hist_ref, z_ref, w_ref, bias_ref, lng_ref, lnb_ref, o_ref, tail_ref, xp_ref,
                  y_ref, *, tb):
    j = pl.program_id(1)

    @pl.when(j == 0)
    def _():
        xp_ref[0:HALO, :] = hist_ref[0, 0]

    @pl.when(j > 0)
    def _():
        hl = halo_ref[...]
        xp_ref[0:HALO, :] = hl[:, :C_B] * _sigmoid(hl[:, C_B:])

    gl = glu_ref[...]
    xp_ref[HALO:HALO + tb, :] = gl[:, :C_B] * _sigmoid(gl[:, C_B:])
    tail_ref[0] = xp_ref[tb:tb + HALO, :]

    rc_n = min(tb, 64)
    cc_n = LANE
    base = HALO - (CONV_B - 1)
    win_n = rc_n + HALO
    for cc in range(C_B // cc_n):
        cs = slice(cc * cc_n, (cc + 1) * cc_n)
        for rc in range(tb // rc_n):
            win = xp_ref[rc * rc_n:rc * rc_n + win_n, cs]
            acc = jnp.zeros((rc_n, cc_n), F32)
            for ph in range(8):
                wph = win if ph == 0 else pltpu.roll(win, win_n - ph, 0)
                for a in range(win_n // 8):
                    k = 8 * a + ph - base
                    if 0 <= k < CONV_B and 8 * a + rc_n <= win_n:
                        acc = acc + w_ref[0, k:k + 1, cs] * wph[8 * a:8 * a + rc_n, :]
            y_ref[rc * rc_n:(rc + 1) * rc_n, cs] = acc

    y = y_ref[...] + bias_ref[0]
    mu = jnp.mean(y, axis=-1, keepdims=True)
    yc = y - mu
    var = jnp.mean(yc * yc, axis=-1, keepdims=True)
    yn = yc * lax.rsqrt(var + EPS) * lng_ref[0] + lnb_ref[0]
    o_ref[...] = (_silu(yn) * _silu(z_ref[...])).astype(BF16)


def _convb(proj, bsz, t_len, hist32, w32, bias, ln_g, ln_b, layer, *, tb):
    sl = layer if hist32.shape[0] > 1 else 0
    nb = t_len // tb
    rows = bsz * t_len
    rb = tb // HALO
    vec = pl.BlockSpec((1, 1, C_B), lambda b, j: (layer, 0, 0))
    kern = functools.partial(_convb_kernel, tb=tb)
    return pl.pallas_call(
        kern,
        grid=(bsz, nb),
        in_specs=[
            pl.BlockSpec((tb, 2 * C_B), lambda b, j: (b * nb + j, COL_GLU // (2 * C_B))),
            pl.BlockSpec((HALO, 2 * C_B), lambda b, j: (jnp.maximum((b * nb + j) * rb - 1, 0), COL_GLU // (2 * C_B))),
            pl.BlockSpec((1, 1, HALO, C_B), lambda b, j: (sl, b, 0, 0)),
            pl.BlockSpec((tb, C_B), lambda b, j: (b * nb + j, COL_ZB // C_B)),
            pl.BlockSpec((1, HALO, C_B), lambda b, j: (layer, 0, 0)),
            vec, vec, vec,
        ],
        out_specs=[
            pl.BlockSpec((tb, C_B), lambda b, j: (b * nb + j, 0)),
            pl.BlockSpec((1, HALO, C_B), lambda b, j: (b, 0, 0)),
        ],
        out_shape=[
            jax.ShapeDtypeStruct((rows, C_B), BF16),
            jax.ShapeDtypeStruct((bsz, HALO, C_B), F32),
        ],
        scratch_shapes=[pltpu.VMEM((HALO + tb, C_B), F32), pltpu.VMEM((tb, C_B), F32)],
        compiler_params=_cparams(("parallel", "arbitrary")),
        name="convb",
    )(proj, proj, hist32, proj, w32, bias, ln_g, ln_b)


SUB = 128
VT_ONES = 16
QK_AHEAD = 3
QK_AHEAD1 = QK_AHEAD + 1


def _tree(parts, op):
    parts = list(parts)
    while len(parts) > 1:
        nxt = [op(parts[n], parts[n + 1]) for n in range(0, len(parts) - 1, 2)]
        if len(parts) % 2:
            nxt.append(parts[-1])
        parts = nxt
    return parts[0]


def _fold8(x, op):
    return _tree([x[r:r + 8, :] for r in range(0, x.shape[0], 8)], op)


def _dsa_t_kernel(qi_ref, wt_ref, ki_ref, q_ref, k_ref, vt_ref, z_ref, o_ref, key_ref, khi_ref, klo_ref, bias_ref,
                  s_ref, p_ref, m_ref, acc_ref, *, tq, n_real, kb, n_keys, n_valid, pos0, topk):
    i = pl.program_id(1)
    nsub = kb // SUB
    last_pos = pos0 + (i + 1) * tq - 1
    lim_last = jnp.minimum((last_pos // CHUNK + 1) * CHUNK, n_valid)
    nkb = jnp.minimum((lim_last + kb - 1) // kb, n_keys // kb)
    qpos = pos0 + i * tq + lax.broadcasted_iota(I32, (1, tq), 1)
    lim = jnp.minimum(((qpos >> 6) + 1) << 6, n_valid)
    sub_iota = lax.broadcasted_iota(I32, (SUB, 1), 0)
    wt = wt_ref[0] * (H_I ** -0.5)

    def p1(kblk, carry):
        for sb in range(nsub):
            r0 = pl.multiple_of(kblk * kb + sb * SUB, SUB)
            kis = ki_ref[0, pl.ds(r0, SUB), :]
            acc = jnp.zeros((SUB, tq), F32)
            for j in range(H_I):
                lg = jnp.dot(kis, qi_ref[0, j * D_I:(j + 1) * D_I, :], preferred_element_type=F32)
                acc = acc + wt[j:j + 1, :] * jnp.maximum(lg, 0.0)
            bits = lax.bitcast_convert_type(jnp.where((r0 + sub_iota) < lim, acc, NEG_INF), I32)
            key = bits ^ ((bits >> 31) & 0x7FFFFFFF)
            ss = slice(sb * SUB, (sb + 1) * SUB)
            key_ref[kblk, ss, :] = key
            khi_ref[kblk, ss, :] = (key >> 16).astype(I16)
            klo_ref[kblk, ss, :] = ((key & 0xFFFF) - 32768).astype(I16)
        return carry

    def p1_pair(it, carry):
        p1(2 * it, carry)
        return p1(2 * it + 1, carry)

    lax.fori_loop(0, nkb // 2, p1_pair, 0)

    @pl.when(nkb % 2 == 1)
    def _():
        p1(nkb - 1, 0)

    def count(pred):
        def body(kblk, cnt):
            parts = [cnt]
            for sb in range(nsub):
                blk = key_ref[kblk, sb * SUB:(sb + 1) * SUB, :]
                parts.append(_fold8(jnp.where(pred(blk, kblk * kb + sb * SUB), 1.0, 0.0), jnp.add))
            return _tree(parts, jnp.add)

        cnt = lax.fori_loop(0, nkb, body, jnp.zeros((8, tq), F32))
        return jnp.sum(cnt, axis=0, keepdims=True)

    def count_ge(cand):
        return count(lambda blk, _: blk >= cand)

    one16 = jnp.ones((16, tq), I16)
    zero16 = jnp.zeros((16, tq), I16)

    def count16_ge(h_ref, cand_s):
        cb = jnp.broadcast_to(cand_s, (16, tq)).astype(I16)

        def body(kblk, cnt):
            parts = [cnt]
            for sb in range(nsub):
                blk = h_ref[kblk, sb * SUB:(sb + 1) * SUB, :]
                parts.append(_tree([jnp.where(blk[r:r + 16, :] >= cb, one16, zero16) for r in range(0, SUB, 16)],
                                   jnp.add))
            return _tree(parts, jnp.add)

        cnt = lax.fori_loop(0, nkb, body, zero16)
        return jnp.sum(cnt.astype(I32).astype(F32), axis=0, keepdims=True)

    def bisect16(h_ref):
        def step(p, t_u):
            cand_u = t_u | jnp.left_shift(jnp.int32(1), 15 - p)
            return jnp.where(count16_ge(h_ref, cand_u - 32768) >= topk, cand_u, t_u)

        return lax.fori_loop(0, 16, step, jnp.zeros((1, tq), I32))

    th_s = bisect16(khi_ref) - 32768
    th16 = jnp.broadcast_to(th_s, (16, tq)).astype(I16)

    def mask_low(kblk, carry):
        for sb in range(nsub):
            for r in range(0, SUB, 16):
                rs = slice(sb * SUB + r, sb * SUB + r + 16)
                hi = khi_ref[kblk, rs, :]
                other = jnp.where(hi > th16, jnp.int16(32767), jnp.int16(-32768))
                klo_ref[kblk, rs, :] = jnp.where(hi == th16, klo_ref[kblk, rs, :], other)
        return carry

    lax.fori_loop(0, nkb, mask_low, 0)
    thr = th_s * 65536 + bisect16(klo_ref)
    thr = jnp.maximum(thr, NEG_INF_KEY + 1)
    cnt_ge = count_ge(thr)

    real_q = lax.broadcasted_iota(I32, (1, tq), 1) < n_real

    @pl.when(jnp.max(jnp.where(real_q, cnt_ge, 0.0)) > topk)
    def _():
        need = topk - count_ge(thr + 1)
        nbits = max(n_keys - 1, 1).bit_length()

        def bis_idx(p, cm):
            cand = cm | jnp.left_shift(jnp.int32(1), nbits - 1 - p)
            below = count(lambda blk, k0: (blk == thr) & ((k0 + sub_iota) < cand))
            return jnp.where(below < need, cand, cm)

        cm = lax.fori_loop(0, nbits, bis_idx, jnp.zeros((1, tq), I32))

        def rewrite(kblk, carry):
            for sb in range(nsub):
                blk = key_ref[kblk, sb * SUB:(sb + 1) * SUB, :]
                drop = (blk == thr) & ((kblk * kb + sb * SUB + sub_iota) > cm)
                key_ref[kblk, sb * SUB:(sb + 1) * SUB, :] = jnp.where(drop, thr - 1, blk)
            return carry

        lax.fori_loop(0, nkb, rewrite, 0)

    m_ref[...] = jnp.full(m_ref.shape, NEG_INF, F32)
    acc_ref[...] = jnp.zeros(acc_ref.shape, F32)
    group = H_C // H_KV
    va = DH_C + VT_ONES

    def p3(kblk, carry):
        k0 = pl.multiple_of(kblk * kb, kb)
        for sb in range(nsub):
            ss = slice(sb * SUB, (sb + 1) * SUB)
            bias_ref[ss, :] = jnp.where(key_ref[kblk, ss, :] >= thr, 0.0, NEG_INF)

        def logits(h):
            g = h // group
            mx = []
            for sb in range(nsub):
                ss = slice(sb * SUB, (sb + 1) * SUB)
                ks = k_ref[0, pl.ds(k0 + sb * SUB, SUB), g * DH_C:(g + 1) * DH_C]
                s = jnp.dot(ks, q_ref[0, h * DH_C:(h + 1) * DH_C, :], preferred_element_type=F32)
                s = s + bias_ref[ss, :]
                s_ref[h % QK_AHEAD1, ss, :] = s
                mx.append(_fold8(s, jnp.maximum))
            return jnp.max(_tree(mx, jnp.maximum), axis=0, keepdims=True)

        m_q = [logits(h) for h in range(QK_AHEAD)]
        for h in range(H_C):
            if h + QK_AHEAD < H_C:
                m_q.append(logits(h + QK_AHEAD))
            m_cur = m_q[h]
            g = h // group
            m_prev = m_ref[h][0:1, :]
            m_new = jnp.maximum(m_prev, m_cur)
            m_safe = jnp.where(m_new == NEG_INF, 0.0, m_new)
            alpha = jnp.exp2((m_prev - m_safe) * Q_SCALE)
            for sb in range(nsub):
                ss = slice(sb * SUB, (sb + 1) * SUB)
                p_ref[ss, :] = jnp.exp2((s_ref[h % QK_AHEAD1, ss, :] - m_safe) * Q_SCALE).astype(BF16)
            pv = jnp.dot(vt_ref[0, kblk, g * va:(g + 1) * va, :], p_ref[...], preferred_element_type=F32)
            acc_ref[h] = alpha * acc_ref[h] + pv
            m_ref[h] = jnp.broadcast_to(m_new, (8, tq))
        return carry

    def p3_pair(it, carry):
        p3(2 * it, carry)
        return p3(2 * it + 1, carry)

    lax.fori_loop(0, nkb // 2, p3_pair, 0)

    @pl.when(nkb % 2 == 1)
    def _():
        p3(nkb - 1, 0)

    for h in range(H_C):
        hs = slice(h * DH_C, (h + 1) * DH_C)
        o = (acc_ref[h, 0:DH_C, :] / acc_ref[h, DH_C:DH_C + 1, :]).T
        o_ref[:, hs] = (o * _silu(z_ref[:, hs])).astype(BF16)


def _dsa_t(z_arr, z_col, qi_t, q_t, wi_t, ki_all, k_all, vt, bsz, t_len, n_valid, pos0, *, tq, n_real, kb):
    n_keys = k_all.shape[1]
    nq = t_len // tq
    rows = bsz * t_len
    topk = min(TOPK, n_valid // 4)
    kern = functools.partial(_dsa_t_kernel, tq=tq, n_real=n_real, kb=kb, n_keys=n_keys, n_valid=n_valid, pos0=pos0,
                             topk=topk)
    return pl.pallas_call(
        kern,
        grid=(bsz, nq),
        in_specs=[
            pl.BlockSpec((1, 512, tq), lambda b, i: (b * nq + i, 0, 0)),
            pl.BlockSpec((1, H_I, tq), lambda b, i: (b * nq + i, 0, 0)),
            pl.BlockSpec((1, n_keys, D_I), lambda b, i: (b, 0, 0)),
            pl.BlockSpec((1, 1024, tq), lambda b, i: (b * nq + i, 0, 0)),
            pl.BlockSpec((1, n_keys, 256), lambda b, i: (b, 0, 0)),
            pl.BlockSpec((1, n_keys // kb, H_KV * (DH_C + VT_ONES), kb), lambda b, i: (b, 0, 0, 0)),
            pl.BlockSpec((tq, 1024), lambda b, i: (b * nq + i, z_col)),
        ],
        out_specs=pl.BlockSpec((tq, 1024), lambda b, i: (b * nq + i, 0)),
        out_shape=jax.ShapeDtypeStruct((rows, 1024), BF16),
        scratch_shapes=[
            pltpu.VMEM((n_keys // kb, kb, tq), I32),
            pltpu.VMEM((n_keys // kb, kb, tq), I16),
            pltpu.VMEM((n_keys // kb, kb, tq), I16),
            pltpu.VMEM((kb, tq), F32),
            pltpu.VMEM((QK_AHEAD1, kb, tq), F32),
            pltpu.VMEM((kb, tq), BF16),
            pltpu.VMEM((H_C, 8, tq), F32),
            pltpu.VMEM((H_C, DH_C + VT_ONES, tq), F32),
        ],
        compiler_params=_cparams(("parallel", "arbitrary")),
        name="dsa_t",
    )(qi_t, wi_t, ki_all, q_t, k_all, vt, z_arr)


def _merge_kernel(oa_ref, ub_ref, oc_ref, gate_ref, x_ref, woa_ref, wpw_ref, woc_ref, wout_ref, fg_ref, xo_ref,
                  *rest, final):
    ya = jnp.dot(oa_ref[...], woa_ref[0], preferred_element_type=F32)
    mix = _sigmoid(gate_ref[:, 0:D_MODEL]) * ya
    yb = jnp.dot(ub_ref[...], wpw_ref[0], preferred_element_type=F32)
    mix = mix + _sigmoid(gate_ref[:, D_MODEL:2 * D_MODEL]) * yb
    yc = jnp.dot(oc_ref[...], woc_ref[0], preferred_element_type=F32)
    mix = mix + _sigmoid(gate_ref[:, 2 * D_MODEL:3 * D_MODEL]) * yc
    xn = x_ref[...] + jnp.dot(mix.astype(BF16), wout_ref[0], preferred_element_type=F32)
    xo_ref[...] = xn
    if final:
        ms = jnp.mean(xn * xn, axis=-1, keepdims=True)
        rest[0][...] = xn * lax.rsqrt(ms + EPS) * fg_ref[...]


def _merge(o_a, u_b, o_c, proj, x2d, w_oa, w_pw, w_oc, w_out, final_g, layer, final):
    rows = x2d.shape[0]
    tm = _pick(rows, (512, 256, 128, 64, 32, 16, 8))
    act = pl.BlockSpec((tm, D_MODEL), lambda i: (i, 0))
    wsp = pl.BlockSpec((1, D_MODEL, D_MODEL), lambda i: (layer, 0, 0))
    out_specs = [act]
    out_shape = [jax.ShapeDtypeStruct((rows, D_MODEL), F32)]
    if final:
        out_specs.append(act)
        out_shape.append(jax.ShapeDtypeStruct((rows, D_MODEL), F32))
    res = pl.pallas_call(
        functools.partial(_merge_kernel, final=final),
        grid=(rows // tm,),
        in_specs=[act, act, act, pl.BlockSpec((tm, 3 * D_MODEL), lambda i: (i, COL_GATE // (3 * D_MODEL))), act,
                  wsp, wsp, wsp, wsp, pl.BlockSpec((1, D_MODEL), lambda i: (0, 0))],
        out_specs=out_specs,
        out_shape=out_shape,
        compiler_params=_cparams(("parallel",)),
        name="merge",
    )(o_a, u_b, o_c, proj, x2d, w_oa, w_pw, w_oc, w_out, final_g)
    return res if final else (res[0], None)


def _permute_w_in(w_in):
    offs = [0]
    for s in IN_SIZES:
        offs.append(offs[-1] + s)
    (qkv, a_in, b_in, z_a, glu, z_b, q_c, k_c, v_c, qi, ki, wi, z_c, gate) = [
        w_in[..., offs[n]:offs[n + 1]] for n in range(len(IN_SIZES))]

    def zeros(n):
        return jnp.zeros(w_in.shape[:-1] + (n,), BF16)

    parts = [qkv, gate, glu, z_a, z_b, q_c, z_c, qi, k_c, v_c, a_in, b_in, wi, zeros(SM_KI - SM_WI - H_I), ki]
    used = sum(p.shape[-1] for p in parts)
    return jnp.concatenate([p.astype(BF16) for p in parts] + [zeros(NP - used)], axis=-1)


def _stream_layer(x2d, layer, final, cfg, state, prm, stacks):
    bsz, t_len, pos0 = cfg["bsz"], cfg["t_len"], cfg["pos0"]
    proj = _inproj(x2d, prm["norm_g"], prm["w_in"], layer)

    o_a, delta_new = _delta(proj, bsz, t_len, state["conv_a8"], state["delta"], prm["conv_a_w"], prm["a_log"],
                            prm["dt_bias"], prm["onorm_a_g"], layer, tb=cfg["delta_tb"], hb=cfg["delta_hb"])
    conv_a_new = proj.reshape(bsz, t_len, NP)[:, t_len - (SHORT_CONV - 1):, COL_QKV:COL_QKV + 3 * D_MODEL]

    u_b, tail_b = _convb(proj, bsz, t_len, state["conv_b32"], prm["conv_b_w32"], prm["conv_b_bias"], prm["ln_b_g"],
                         prm["ln_b_b"], layer, tb=cfg["convb_tb"])
    conv_b_new = tail_b[:, HALO - (CONV_B - 1):, :]

    n_valid = pos0 + t_len
    kb = cfg["dsa_kb"]
    tq = cfg["dsa_tq"]
    n_keys = -(-n_valid // kb) * kb
    direct = tq == cfg["dsa_tq_real"] and pos0 == 0 and n_keys == n_valid
    q_r, qi_r, kst, vst, kist, k_bf, ki_bf, v_x = _rope(
        proj, stacks, layer, bsz, t_len, pos0, tb=kb if direct else _pick(t_len, (512, 256, 128, 64, 32)),
        tq_t=tq if direct else None)
    wi = proj[:, COL_SMALL + SM_WI:COL_SMALL + SM_WI + H_I]
    if direct:
        wi_t = wi.reshape(bsz * t_len // tq, tq, H_I).transpose(0, 2, 1)
        o_c = _dsa_t(proj, COL_ZC // 1024, qi_r, q_r, wi_t, ki_bf, k_bf, v_x, bsz, t_len, n_valid, pos0, tq=tq,
                     n_real=tq, kb=kb)
    else:
        def with_past(past, new, width):
            parts = [past, new] if pos0 else [new]
            if n_keys > n_valid:
                parts.append(jnp.zeros((bsz, n_keys - n_valid, width), BF16))
            return jnp.concatenate(parts, axis=1) if len(parts) > 1 else new

        def pad_q(a):
            a = a.reshape(bsz, t_len, a.shape[-1])
            return jnp.pad(a, ((0, 0), (0, tq - t_len), (0, 0))).reshape(bsz * tq, a.shape[-1])

        def pad_qt(a):
            return pad_q(a).reshape(bsz, tq, a.shape[-1]).transpose(0, 2, 1)

        k_all = with_past(state.get("past_k"), k_bf, 256)
        ki_all = with_past(state.get("past_ki"), ki_bf, D_I)
        vt = with_past(state.get("past_v"), v_x, 256).reshape(bsz, n_keys // kb, kb, H_KV, DH_C).transpose(0, 1, 3, 4, 2)
        vt = jnp.concatenate([vt, jnp.ones((bsz, n_keys // kb, H_KV, VT_ONES, kb), BF16)], axis=3)
        vt = vt.reshape(bsz, n_keys // kb, H_KV * (DH_C + VT_ONES), kb)
        o_c = _dsa_t(pad_q(proj[:, COL_ZC:COL_ZC + 1024]), 0, pad_qt(qi_r), pad_qt(q_r), pad_qt(wi), ki_all, k_all, vt, bsz, tq,
                     n_valid, pos0, tq=tq, n_real=t_len, kb=kb)
        o_c = o_c.reshape(bsz, tq, 1024)[:, :t_len].reshape(bsz * t_len, 1024)

    x_new, y_fin = _merge(o_a, u_b, o_c, proj, x2d, prm["w_o_a"], prm["w_pw2_b"], prm["w_o_c"], prm["w_out"],
                          prm["final_norm_g"], layer, final)
    return x_new, y_fin, (conv_a_new, delta_new, conv_b_new), (kst, vst, kist)


def _run(x_prompt, x_sample, cache_k, cache_v, cache_idx_k, state_conv_a, state_delta, state_conv_b, prm, cfg_p,
         cfg_s):
    n_layers = prm["norm_g"].shape[0]
    bp, tp = cfg_p["bsz"], cfg_p["t_len"]
    bs, ts = cfg_s["bsz"], cfg_s["t_len"]
    past = cfg_s["pos0"]
    xp = x_prompt.reshape(bp * tp, D_MODEL)
    xs = x_sample.reshape(bs * ts, D_MODEL)
    zero_state = {
        "conv_a8": jnp.zeros((1, bp, 8, 3 * D_MODEL), F32),
        "delta": jnp.zeros((1, bp, H_A, DK_A, DK_A), F32),
        "conv_b32": jnp.zeros((1, bp, HALO, C_B), F32),
    }
    conv_a8_s = jnp.pad(state_conv_a, ((0, 0), (0, 0), (8 - (SHORT_CONV - 1), 0), (0, 0)))
    conv_b32_s = jnp.pad(state_conv_b, ((0, 0), (0, 0), (HALO - (CONV_B - 1), 0), (0, 0)))
    past_k = cache_k.reshape(n_layers, bs, past, 256).astype(BF16)
    past_v = cache_v.reshape(n_layers, bs, past, 256).astype(BF16)
    past_ki = cache_idx_k.astype(BF16)

    def empty_stacks(bsz, t_len):
        return (jnp.zeros((n_layers, bsz, t_len, 256), F32), jnp.zeros((n_layers, bsz, t_len, 256), F32),
                jnp.zeros((n_layers, bsz, t_len, D_I), F32))

    stk_p, stk_s = empty_stacks(bp, tp), empty_stacks(bs, ts)
    st_p, st_s = [], []
    yp = ys = None
    for layer in range(n_layers):
        final = layer == n_layers - 1
        xp, yp, sp, stk_p = _stream_layer(xp, layer, final, cfg_p, zero_state, prm, stk_p)
        s_state = {"conv_a8": conv_a8_s, "delta": state_delta, "conv_b32": conv_b32_s,
                   "past_k": past_k[layer], "past_v": past_v[layer], "past_ki": past_ki[layer]}
        xs, ys, ss, stk_s = _stream_layer(xs, layer, final, cfg_s, s_state, prm, stk_s)
        st_p.append(sp)
        st_s.append(ss)

    def outputs(stk, states, bsz, t_len):
        kst, vst, kist = stk
        return (kst.reshape(n_layers, bsz, t_len, H_KV, DH_C), vst.reshape(n_layers, bsz, t_len, H_KV, DH_C), kist
                ) + tuple(jnp.stack([st[n] for st in states], axis=0) for n in range(3))

    return ((yp.reshape(bp, tp, D_MODEL), ys.reshape(bs, ts, D_MODEL))
            + outputs(stk_p, st_p, bp, tp) + outputs(stk_s, st_s, bs, ts))


def _prep_params(norm_g, w_in, conv_a_w, a_log, dt_bias, onorm_a_g, w_o_a, conv_b_w, conv_b_bias, ln_b_g, ln_b_b,
                 w_pw2_b, w_o_c, w_out, final_norm_g):
    return {
        "norm_g": norm_g[:, None, :], "w_in": _permute_w_in(w_in), "conv_a_w": conv_a_w, "a_log": a_log, "dt_bias": dt_bias,
        "onorm_a_g": onorm_a_g[:, None, :], "w_o_a": w_o_a.astype(BF16),
        "conv_b_w32": jnp.pad(conv_b_w, ((0, 0), (0, HALO - CONV_B), (0, 0))),
        "conv_b_bias": conv_b_bias[:, None, :], "ln_b_g": ln_b_g[:, None, :], "ln_b_b": ln_b_b[:, None, :],
        "w_pw2_b": w_pw2_b.astype(BF16),
        "w_o_c": w_o_c.astype(BF16), "w_out": w_out.astype(BF16), "final_norm_g": final_norm_g[None, :],
    }


def _stream_cfg(bsz, t_len, pos0):
    c = CHUNK if t_len % CHUNK == 0 else t_len
    n_valid = pos0 + t_len
    if t_len % 256 == 0:
        tq, kb = 256, 512
    else:
        tq = -(-t_len // LANE) * LANE
        kb = 384 if n_valid > 384 else LANE
    delta_tb = _pick(t_len, (512, 256, 128, 64, t_len))
    if delta_tb % c:
        delta_tb = t_len
    return {
        "bsz": bsz, "t_len": t_len, "pos0": pos0,
        "delta_tb": delta_tb, "delta_hb": H_A,
        "convb_tb": _pick(t_len, (256, 128, 64, 32)),
        "dsa_tq": tq, "dsa_tq_real": min(tq, t_len), "dsa_kb": kb,
    }


def kernel(x_prompt, x_sample, cache_k, cache_v, cache_idx_k, state_conv_a, state_delta, state_conv_b, norm_g, w_in,
           conv_a_w, a_log, dt_bias, onorm_a_g, w_o_a, conv_b_w, conv_b_bias, ln_b_g, ln_b_b, w_pw2_b, w_o_c, w_out,
           final_norm_g):
    prm = _prep_params(norm_g, w_in, conv_a_w, a_log, dt_bias, onorm_a_g, w_o_a, conv_b_w, conv_b_bias, ln_b_g,
                       ln_b_b, w_pw2_b, w_o_c, w_out, final_norm_g)
    cfg_p = _stream_cfg(x_prompt.shape[0], x_prompt.shape[1], 0)
    cfg_s = _stream_cfg(x_sample.shape[0], x_sample.shape[1], cache_k.shape[2])
    return _run(x_prompt, x_sample, cache_k, cache_v, cache_idx_k, state_conv_a, state_delta, state_conv_b, prm,
                cfg_p, cfg_s)
```
